```python
import jax, jax.numpy as jnp
from jax import lax
import numpy as np

D_MODEL = 1024
BATCH = 8
SEQ = 4096
DEPTH = 4

GRID_W = 64
CTX_LEN = 256
N_EVEN = (DEPTH + 1) // 2
N_ODD = DEPTH // 2
MIX_WIDTH = D_MODEL

RW_WIDTH = MIX_WIDTH // 2
RW_HD = 64
RW_HEADS = RW_WIDTH // RW_HD
RW_DECAY_RANK = 32
RW_A_RANK = 32
RW_GATE_RANK = 64
RW_GN_EPS = 64e-5
RW_COLS = 3 * RW_WIDTH + RW_DECAY_RANK + RW_A_RANK + RW_GATE_RANK
GLA_HEADS = 4
GLA_DV = (MIX_WIDTH - RW_WIDTH) // GLA_HEADS
GLA_DK = GLA_DV // 2
GLA_GATE_RANK = 16
GLA_GATE_NORM = 16.0
GLA_CHUNK = 64
GLA_COLS = 2 * GLA_HEADS * GLA_DK + 2 * GLA_HEADS * GLA_DV + GLA_GATE_RANK
EVEN_COLS = RW_COLS + GLA_COLS
SSD_INNER = MIX_WIDTH // 2
SSD_P = 64
SSD_HEADS = SSD_INNER // SSD_P
SSD_N = 128
SSD_GROUPS = 2
SSD_CHUNK = 128
SSD_CONV_CH = SSD_INNER + 2 * SSD_GROUPS * SSD_N
SSD_COLS = SSD_INNER + SSD_CONV_CH + SSD_HEADS
LRU_WIDTH = MIX_WIDTH - SSD_INNER
LRU_BLOCKS = 8
LRU_BD = LRU_WIDTH // LRU_BLOCKS
LRU_C = 8.0
LRU_COLS = 2 * LRU_WIDTH
ODD_COLS = SSD_COLS + LRU_COLS
CONV_W = 4
PEER_HEADS = 8
N_KEYS = 128
N_EXPERTS = N_KEYS * N_KEYS
PEER_TOPK = 16
PEER_DQ = 256
PEER_BLOCK = 128

kernel_name = "hybrid_rwkv7_gla_ssd_rglru_peer_dit"


def rmsnorm(x, g, eps=1e-6):
    xf = x.astype(jnp.float32)
    y = xf * lax.rsqrt(jnp.mean(xf * xf, -1, keepdims=True) + eps)
    return (y * g.astype(jnp.float32)).astype(x.dtype)


def modulate(x, g, shift, scale):
    return rmsnorm(x, g) * (1 + scale) + shift


def split_cols(p, sizes):
    return jnp.split(p, [int(i) for i in np.cumsum(sizes)], axis=-1)


def flip(t):
    return jnp.flip(t, axis=1)


def grid_to_cols(x):
    B, T, C = x.shape
    rows = T // GRID_W
    return x.reshape(B, rows, GRID_W, C).transpose(0, 2, 1, 3).reshape(B, T, C)


def cols_to_grid(x):
    B, T, C = x.shape
    rows = T // GRID_W
    return x.reshape(B, GRID_W, rows, C).transpose(0, 2, 1, 3).reshape(B, T, C)


def qshift_grid(x):
    B, T, C = x.shape
    rows = T // GRID_W
    g = jnp.pad(x.reshape(B, rows, GRID_W, C), ((0, 0), (1, 1), (1, 1), (0, 0)))
    sel = jnp.arange(C) % 4
    out = jnp.where(sel == 0, g[:, 1:-1, :-2], jnp.where(sel == 1, g[:, 1:-1, 2:],
                    jnp.where(sel == 2, g[:, :-2, 1:-1], g[:, 2:, 1:-1])))
    return out.reshape(B, T, C)


def shift_seq(x):
    prev = jnp.pad(x, ((0, 0), (1, 0), (0, 0)))[:, :-1]
    nxt = jnp.pad(x, ((0, 0), (0, 1), (0, 0)))[:, 1:]
    return jnp.where(jnp.arange(x.shape[-1]) % 2 == 0, prev, nxt)


def dwconv(x, w, b):
    y = lax.conv_general_dilated(x, w[:, None, :].astype(x.dtype), (1,), [((CONV_W - 1) // 2, CONV_W // 2)],
                                 dimension_numbers=("NWC", "WIO", "NWC"), feature_group_count=x.shape[-1])
    return y + b


def rwkv7_scan(r, w, k, v, kk, a, S0):
    xs = tuple(jnp.moveaxis(t.astype(jnp.float32), 1, 0) for t in (r, w, k, v, kk, a))

    def step(S, inp):
        r_t, w_t, k_t, v_t, kk_t, a_t = inp
        sk = jnp.einsum("bhvk,bhk->bhv", S, kk_t)
        S = S * w_t[:, :, None, :] - sk[..., None] * (kk_t * a_t)[:, :, None, :] + v_t[..., None] * k_t[:, :, None, :]
        return S, jnp.einsum("bhvk,bhk->bhv", S, r_t)

    S, ys = lax.scan(step, S0, xs)
    return jnp.moveaxis(ys, 0, 1).astype(r.dtype), S


def rwkv_stream(p, S0f, S0b, w0, w_up, a0, a_up, g_up, k_k, k_a, r_k, ln_g, ln_b):
    B, T, _ = p.shape
    r, k, v, wd, ad, gd = split_cols(p, [RW_WIDTH, RW_WIDTH, RW_WIDTH, RW_DECAY_RANK, RW_A_RANK])
    heads = lambda t: t.reshape(B, T, RW_HEADS, RW_HD)
    kk = heads(k * k_k).astype(jnp.float32)
    kk = kk * lax.rsqrt(jnp.maximum(jnp.sum(kk * kk, -1, keepdims=True), 1e-12))
    g = jax.nn.sigmoid(gd) @ g_up
    ys, states, bonus = [], [], []
    for d, S0 in enumerate((S0f, S0b)):
        wl = (w0[d] + jnp.tanh(wd) @ w_up[d]).astype(jnp.float32)
        decay = jnp.exp(-jnp.exp(-jax.nn.softplus(-wl) - 0.5))
        a = jax.nn.sigmoid(a0[d] + ad @ a_up[d])
        kd = k * (1 + (a - 1) * k_a)
        args = [heads(r), heads(decay), heads(kd), heads(v), kk, heads(a)]
        if d == 1:
            args = [flip(t) for t in args]
        yd, Sd = rwkv7_scan(*args, S0)
        ys.append(flip(yd) if d == 1 else yd)
        states.append(Sd)
        bonus.append(jnp.sum(heads(r) * heads(kd) * r_k, -1, keepdims=True) * heads(v))
    y = (ys[0] + ys[1]).astype(jnp.float32)
    mu = jnp.mean(y, -1, keepdims=True)
    var = jnp.mean(jnp.square(y - mu), -1, keepdims=True)
    yn = ((y - mu) * lax.rsqrt(var + RW_GN_EPS)).reshape(B, T, RW_WIDTH) * ln_g + ln_b
    out = (yn + (bonus[0] + bonus[1]).reshape(B, T, RW_WIDTH)) * g
    return out.astype(p.dtype), states[0], states[1]


def gla_chunked(q, k, v, g, S0):
    B, T, H, K = q.shape
    V = v.shape[-1]
    L = GLA_CHUNK
    n = T // L
    chunks = lambda t: t.astype(jnp.float32).reshape(B, n, L, H, t.shape[-1]).transpose(1, 0, 3, 2, 4)
    mask = jnp.tril(jnp.ones((L, L), bool))

    def step(S, inp):
        qc, kc, vc, gc = inp
        b = jnp.cumsum(gc, axis=2)
        diff = b[:, :, :, None, :] - b[:, :, None, :, :]
        dec = jnp.exp(jnp.where(mask[:, :, None], diff, -jnp.inf))
        A = jnp.einsum("bhik,bhjk,bhijk->bhij", qc, kc, dec)
        o = A @ vc + jnp.einsum("bhik,bhkv->bhiv", qc * jnp.exp(b), S)
        b_last = b[:, :, -1:, :]
        S = jnp.exp(b_last[:, :, 0, :])[..., None] * S + jnp.einsum("bhjk,bhjv->bhkv", kc * jnp.exp(b_last - b), vc)
        return S, o

    S, o = lax.scan(step, S0, (chunks(q), chunks(k), chunks(v), chunks(g)))
    return o.transpose(1, 0, 3, 2, 4).reshape(B, T, H, V).astype(q.dtype), S


def gla_stream(p, S0f, S0b, gate_up, gate_b, norm_g):
    B, T, _ = p.shape
    q, k, v, gd, og = split_cols(p, [GLA_HEADS * GLA_DK, GLA_HEADS * GLA_DK, GLA_HEADS * GLA_DV, GLA_GATE_RANK])
    q = q.reshape(B, T, GLA_HEADS, GLA_DK) * GLA_DK ** -0.5
    k = k.reshape(B, T, GLA_HEADS, GLA_DK)
    v = v.reshape(B, T, GLA_HEADS, GLA_DV)
    outs, states = [], []
    for d, S0 in enumerate((S0f, S0b)):
        glog = jax.nn.log_sigmoid((gd @ gate_up[d] + gate_b[d]).astype(jnp.float32)) / GLA_GATE_NORM
        args = [q, k, v, glog.reshape(B, T, GLA_HEADS, GLA_DK)]
        if d == 1:
            args = [flip(t) for t in args]
        od, Sd = gla_chunked(*args, S0)
        outs.append(flip(od) if d == 1 else od)
        states.append(Sd)
    o = rmsnorm(outs[0] + outs[1], norm_g.reshape(GLA_HEADS, GLA_DV)).reshape(B, T, GLA_HEADS * GLA_DV)
    return (o * jax.nn.silu(og)).astype(p.dtype), states[0], states[1]


def ssd_chunked(x, dt, A, Bm, Cm, S0):
    Bsz, T, H, P = x.shape
    N = Bm.shape[-1]
    L = SSD_CHUNK
    n = T // L
    f = lambda t: t.astype(jnp.float32)
    xc = (f(x) * f(dt)[..., None]).reshape(Bsz, n, L, H, P)
    Bc = f(Bm).reshape(Bsz, n, L, H, N)
    Cc = f(Cm).reshape(Bsz, n, L, H, N)
    acs = jnp.cumsum((f(dt) * A).reshape(Bsz, n, L, H).transpose(0, 3, 1, 2), -1)
    mask = jnp.tril(jnp.ones((L, L), bool))
    Lmat = jnp.exp(jnp.where(mask, acs[..., :, None] - acs[..., None, :], -jnp.inf))
    G = jnp.einsum("bclhn,bcshn->bhcls", Cc, Bc) * Lmat
    y_diag = jnp.einsum("bhcls,bcshp->bclhp", G, xc)
    contrib = jnp.einsum("bclhn,bhcl,bclhp->bchpn", Bc, jnp.exp(acs[..., -1:] - acs), xc)
    chunk_decay = jnp.exp(acs[..., -1])

    def step(S, inp):
        dec, st = inp
        return dec[..., None, None] * S + st, S

    S, S_in = lax.scan(step, S0, (jnp.moveaxis(chunk_decay, 2, 0), jnp.moveaxis(contrib, 1, 0)))
    y_off = jnp.einsum("bclhn,bchpn,bhcl->bclhp", Cc, jnp.moveaxis(S_in, 0, 1), jnp.exp(acs))
    return (y_diag + y_off).reshape(Bsz, T, H, P).astype(x.dtype), S


def ssd_stream(p, S0f, S0b, conv_w, conv_b, dt_bias, A_log, D_skip, norm_g):
    B, T, _ = p.shape
    z, xbc, dtr = split_cols(p, [SSD_INNER, SSD_CONV_CH])
    xbc = jax.nn.silu(dwconv(xbc, conv_w, conv_b))
    xs, Bm, Cm = split_cols(xbc, [SSD_INNER, SSD_GROUPS * SSD_N])
    xs = xs.reshape(B, T, SSD_HEADS, SSD_P)
    rep = SSD_HEADS // SSD_GROUPS
    Bm = jnp.repeat(Bm.reshape(B, T, SSD_GROUPS, SSD_N), rep, axis=2)
    Cm = jnp.repeat(Cm.reshape(B, T, SSD_GROUPS, SSD_N), rep, axis=2)
    ys, states = [], []
    for d, S0 in enumerate((S0f, S0b)):
        dt = jax.nn.softplus((dtr + dt_bias[d]).astype(jnp.float32))
        A = -jnp.exp(A_log[d].astype(jnp.float32))
        args = [xs, dt, Bm, Cm]
        if d == 1:
            args = [flip(t) for t in args]
        yd, Sd = ssd_chunked(args[0], args[1], A, args[2], args[3], S0)
        ys.append(flip(yd) if d == 1 else yd)
        states.append(Sd)
    y = (ys[0] + ys[1] + xs * D_skip[:, None]).reshape(B, T, SSD_INNER) * jax.nn.silu(z)
    return rmsnorm(y, norm_g).astype(p.dtype), states[0], states[1]


def linear_scan(a, b, h0):
    b = b.at[:, 0].add(a[:, 0] * h0)
    comb = lambda l, r: (l[0] * r[0], r[0] * l[1] + r[1])
    _, h = lax.associative_scan(comb, (a, b), axis=1)
    return h, h[:, -1]


def lru_stream(p, h0f, h0b, conv_w, conv_b, wa, ba, wx, bx, lam):
    B, T, _ = p.shape
    xr, gate = split_cols(p, [LRU_WIDTH])
    xf = dwconv(xr, conv_w, conv_b).astype(jnp.float32)
    xh = xf.reshape(B, T, LRU_BLOCKS, LRU_BD)
    hs, finals = [], []
    for d, h0 in enumerate((h0f, h0b)):
        r = jax.nn.sigmoid(jnp.einsum("btgi,gij->btgj", xh, wa[d]).reshape(B, T, LRU_WIDTH) + ba[d])
        i = jax.nn.sigmoid(jnp.einsum("btgi,gij->btgj", xh, wx[d]).reshape(B, T, LRU_WIDTH) + bx[d])
        log_a = -LRU_C * r * jax.nn.softplus(-lam[d])
        a = jnp.exp(log_a)
        b = jnp.sqrt(-jnp.expm1(2 * log_a)) * (i * xf)
        if d == 1:
            a, b = flip(a), flip(b)
        h, hT = linear_scan(a, b, h0)
        hs.append(flip(h) if d == 1 else h)
        finals.append(hT)
    y = (hs[0] + hs[1]).astype(p.dtype) * jax.nn.gelu(gate)
    return y, finals[0], finals[1]


def even_mixer(u, uc, w_in, mu, rw_p, gla_p):
    B = u.shape[0]
    p, pc = u @ w_in, uc @ w_in
    rw, gl = p[..., :RW_COLS], p[..., RW_COLS:]
    rwc, glc = pc[..., :RW_COLS], pc[..., RW_COLS:]
    rw = rw + mu * (qshift_grid(rw) - rw)
    rwc = rwc + mu * (shift_seq(rwc) - rwc)
    z_rw = jnp.zeros((B, RW_HEADS, RW_HD, RW_HD), jnp.float32)
    yc_rw, Sf, Sb = rwkv_stream(rwc, z_rw, z_rw, *rw_p)
    y_rw, _, _ = rwkv_stream(rw, Sf, Sb, *rw_p)
    z_gla = jnp.zeros((B, GLA_HEADS, GLA_DK, GLA_DV), jnp.float32)
    yc_gl, Gf, Gb = gla_stream(glc, z_gla, z_gla, *gla_p)
    y_gl, _, _ = gla_stream(gl, Gf, Gb, *gla_p)
    return jnp.concatenate([y_rw, y_gl], -1), jnp.concatenate([yc_rw, yc_gl], -1)


def odd_mixer(u, uc, w_in, ssd_p, lru_p):
    B = u.shape[0]
    p = grid_to_cols(u @ w_in)
    pc = uc @ w_in
    z_s = jnp.zeros((B, SSD_HEADS, SSD_P, SSD_N), jnp.float32)
    yc_s, Sf, Sb = ssd_stream(pc[..., :SSD_COLS], z_s, z_s, *ssd_p)
    y_s, _, _ = ssd_stream(p[..., :SSD_COLS], Sf, Sb, *ssd_p)
    z_l = jnp.zeros((B, LRU_WIDTH), jnp.float32)
    yc_l, hf, hb = lru_stream(pc[..., SSD_COLS:], z_l, z_l, *lru_p)
    y_l, _, _ = lru_stream(p[..., SSD_COLS:], hf, hb, *lru_p)
    return cols_to_grid(jnp.concatenate([y_s, y_l], -1)), jnp.concatenate([yc_s, yc_l], -1)


def peer(x, wq, keys, u_tab, v_tab):
    B, T, D = x.shape
    M = B * T
    xf = x.reshape(M, D)
    q = (xf @ wq).astype(jnp.float32).reshape(M, PEER_HEADS, 2, PEER_DQ // 2)
    s = jnp.einsum("mhsd,hskd->mhsk", q, keys.astype(jnp.float32))
    top_s, top_i = lax.top_k(s, PEER_TOPK)
    cand_s = (top_s[:, :, 0, :, None] + top_s[:, :, 1, None, :]).reshape(M, PEER_HEADS, PEER_TOPK * PEER_TOPK)
    cand_i = (top_i[:, :, 0, :, None] * N_KEYS + top_i[:, :, 1, None, :]).reshape(M, PEER_HEADS, PEER_TOPK * PEER_TOPK)
    best_s, pos = lax.top_k(cand_s, PEER_TOPK)
    idx = jnp.take_along_axis(cand_i, pos, axis=-1)
    gate = jax.nn.softmax(best_s, axis=-1).astype(x.dtype)
    nb = M // PEER_BLOCK
    E = PEER_HEADS * PEER_TOPK

    def block(args):
        xs, ids, gs = args
        act = jax.nn.gelu(jnp.einsum("md,med->me", xs, u_tab[ids]))
        return jnp.einsum("me,med->md", act * gs, v_tab[ids])

    out = lax.map(block, (xf.reshape(nb, PEER_BLOCK, D), idx.reshape(nb, PEER_BLOCK, E), gate.reshape(nb, PEER_BLOCK, E)))
    return out.reshape(B, T, D)


def setup_inputs(seed: int = 0) -> dict:
    key = jax.random.key(seed)
    ks = iter(jax.random.split(key, 64))
    nrm = lambda shape, scale: jax.random.normal(next(ks), shape, jnp.float32) * scale
    unif = lambda shape, lo, hi: jax.random.uniform(next(ks), shape, jnp.float32, lo, hi)
    D = D_MODEL
    inp = {}
    inp["x"] = nrm((BATCH, SEQ, D), 1.0)
    inp["c"] = nrm((BATCH, D), 1.0)
    inp["ctx"] = nrm((BATCH, CTX_LEN, D), 1.0)
    inp["c_ctx"] = nrm((D,), 1.0)
    inp["ada_w"] = nrm((DEPTH, D, 6 * D), 0.5 * D ** -0.5)
    inp["ada_b"] = nrm((DEPTH, 6 * D), 0.02)
    inp["norm1_g"] = 1.0 + nrm((DEPTH, D), 0.02)
    inp["norm2_g"] = 1.0 + nrm((DEPTH, D), 0.02)
    inp["ev_w_in"] = nrm((N_EVEN, D, EVEN_COLS), D ** -0.5)
    inp["rw_mu"] = unif((N_EVEN, RW_COLS), 0.0, 1.0)
    inp["rw_w0"] = unif((N_EVEN, 2, RW_WIDTH), -6.0, 1.0)
    inp["rw_w_up"] = nrm((N_EVEN, 2, RW_DECAY_RANK, RW_WIDTH), 0.5 * RW_DECAY_RANK ** -0.5)
    inp["rw_a0"] = nrm((N_EVEN, 2, RW_WIDTH), 0.5)
    inp["rw_a_up"] = nrm((N_EVEN, 2, RW_A_RANK, RW_WIDTH), 0.5 * RW_A_RANK ** -0.5)
    inp["rw_g_up"] = nrm((N_EVEN, RW_GATE_RANK, RW_WIDTH), RW_GATE_RANK ** -0.5)
    inp["rw_k_k"] = 0.85 + nrm((N_EVEN, RW_WIDTH), 0.05)
    inp["rw_k_a"] = 1.0 + nrm((N_EVEN, RW_WIDTH), 0.05)
    inp["rw_r_k"] = nrm((N_EVEN, RW_HEADS, RW_HD), 0.1)
    inp["rw_ln_g"] = 1.0 + nrm((N_EVEN, RW_WIDTH), 0.02)
    inp["rw_ln_b"] = nrm((N_EVEN, RW_WIDTH), 0.02)
    inp["gla_gate_up"] = nrm((N_EVEN, 2, GLA_GATE_RANK, GLA_HEADS * GLA_DK), GLA_GATE_RANK ** -0.5)
    inp["gla_gate_b"] = unif((N_EVEN, 2, GLA_HEADS * GLA_DK), -1.0, 4.0)
    inp["gla_norm_g"] = 1.0 + nrm((N_EVEN, GLA_HEADS * GLA_DV), 0.02)
    inp["od_w_in"] = nrm((N_ODD, D, ODD_COLS), D ** -0.5)
    inp["ssd_conv_w"] = nrm((N_ODD, CONV_W, SSD_CONV_CH), CONV_W ** -0.5)
    inp["ssd_conv_b"] = nrm((N_ODD, SSD_CONV_CH), 0.02)
    dt0 = jnp.exp(unif((N_ODD, 2, SSD_HEADS), float(np.log(1e-3)), float(np.log(1e-1))))
    inp["ssd_dt_bias"] = dt0 + jnp.log(-jnp.expm1(-dt0))
    inp["ssd_A_log"] = jnp.log(unif((N_ODD, 2, SSD_HEADS), 1.0, 16.0))
    inp["ssd_D"] = 1.0 + nrm((N_ODD, SSD_HEADS), 0.1)
    inp["ssd_norm_g"] = 1.0 + nrm((N_ODD, SSD_INNER), 0.02)
    inp["lru_conv_w"] = nrm((N_ODD, CONV_W, LRU_WIDTH), CONV_W ** -0.5)
    inp["lru_conv_b"] = nrm((N_ODD, LRU_WIDTH), 0.02)
    inp["lru_wa"] = nrm((N_ODD, 2, LRU_BLOCKS, LRU_BD, LRU_BD), LRU_BD ** -0.5)
    inp["lru_ba"] = nrm((N_ODD, 2, LRU_WIDTH), 0.1)
    inp["lru_wx"] = nrm((N_ODD, 2, LRU_BLOCKS, LRU_BD, LRU_BD), LRU_BD ** -0.5)
    inp["lru_bx"] = nrm((N_ODD, 2, LRU_WIDTH), 0.1)
    a_base = unif((N_ODD, 2, LRU_WIDTH), 0.9, 0.999) ** (1.0 / LRU_C)
    inp["lru_lam"] = jnp.log(a_base) - jnp.log1p(-a_base)
    inp["w_out"] = nrm((DEPTH, MIX_WIDTH, D), MIX_WIDTH ** -0.5)
    inp["peer_wq"] = nrm((DEPTH, D, PEER_HEADS * PEER_DQ), D ** -0.5)
    inp["peer_keys"] = nrm((DEPTH, PEER_HEADS, 2, N_KEYS, PEER_DQ // 2), (PEER_DQ // 2) ** -0.5)
    inp["peer_u"] = nrm((DEPTH, N_EXPERTS, D), D ** -0.5)
    inp["peer_v"] = nrm((DEPTH, N_EXPERTS, D), 0.5)
    inp["final_g"] = 1.0 + nrm((D,), 0.02)
    return inp


def reference(x, c, ctx, c_ctx, ada_w, ada_b, norm1_g, norm2_g, ev_w_in, rw_mu, rw_w0, rw_w_up, rw_a0, rw_a_up,
              rw_g_up, rw_k_k, rw_k_a, rw_r_k, rw_ln_g, rw_ln_b, gla_gate_up, gla_gate_b, gla_norm_g, od_w_in,
              ssd_conv_w, ssd_conv_b, ssd_dt_bias, ssd_A_log, ssd_D, ssd_norm_g, lru_conv_w, lru_conv_b, lru_wa,
              lru_ba, lru_wx, lru_bx, lru_lam, w_out, peer_wq, peer_keys, peer_u, peer_v, final_g):
    h, hc = x, ctx
    sc, scc = jax.nn.silu(c), jax.nn.silu(c_ctx)
    for l in range(DEPTH):
        last = l == DEPTH - 1
        i = l // 2
        mod = jnp.split((sc @ ada_w[l] + ada_b[l])[:, None, :], 6, axis=-1)
        modc = jnp.split(scc @ ada_w[l] + ada_b[l], 6, axis=-1)
        u = modulate(h, norm1_g[l], mod[0], mod[1])
        uc = modulate(hc, norm1_g[l], modc[0], modc[1])
        if l % 2 == 0:
            rw_p = (rw_w0[i], rw_w_up[i], rw_a0[i], rw_a_up[i], rw_g_up[i], rw_k_k[i], rw_k_a[i], rw_r_k[i],
                    rw_ln_g[i], rw_ln_b[i])
            gla_p = (gla_gate_up[i], gla_gate_b[i], gla_norm_g[i])
            y, yc = even_mixer(u, uc, ev_w_in[i], rw_mu[i], rw_p, gla_p)
        else:
            ssd_p = (ssd_conv_w[i], ssd_conv_b[i], ssd_dt_bias[i], ssd_A_log[i], ssd_D[i], ssd_norm_g[i])
            lru_p = (lru_conv_w[i], lru_conv_b[i], lru_wa[i], lru_ba[i], lru_wx[i], lru_bx[i], lru_lam[i])
            y, yc = odd_mixer(u, uc, od_w_in[i], ssd_p, lru_p)
        h = h + mod[2] * (y @ w_out[l])
        u = modulate(h, norm2_g[l], mod[3], mod[4])
        h = h + mod[5] * peer(u, peer_wq[l], peer_keys[l], peer_u[l], peer_v[l])
        if not last:
            hc = hc + modc[2] * (yc @ w_out[l])
            uc = modulate(hc, norm2_g[l], modc[3], modc[4])
            hc = hc + modc[5] * peer(uc, peer_wq[l], peer_keys[l], peer_u[l], peer_v[l])
    return rmsnorm(h, final_g)
```

```python
import functools

import jax
import jax.numpy as jnp
from jax import lax
import numpy as np
from jax.experimental import pallas as pl
from jax.experimental.pallas import tpu as pltpu

D_MODEL = 1024
DEPTH = 4
GRID_W = 64
MIX_WIDTH = D_MODEL

RW_WIDTH = MIX_WIDTH // 2
RW_HD = 64
RW_HEADS = RW_WIDTH // RW_HD
RW_DECAY_RANK = 32
RW_A_RANK = 32
RW_GATE_RANK = 64
RW_GN_EPS = 64e-5
RW_COLS = 3 * RW_WIDTH + RW_DECAY_RANK + RW_A_RANK + RW_GATE_RANK
GLA_HEADS = 4
GLA_DV = (MIX_WIDTH - RW_WIDTH) // GLA_HEADS
GLA_DK = GLA_DV // 2
GLA_GATE_RANK = 16
GLA_GATE_NORM = 16.0
GLA_CHUNK = 64
SSD_INNER = MIX_WIDTH // 2
SSD_P = 64
SSD_HEADS = SSD_INNER // SSD_P
SSD_N = 128
SSD_GROUPS = 2
SSD_CHUNK = 128
SSD_CONV_CH = SSD_INNER + 2 * SSD_GROUPS * SSD_N
SSD_COLS = SSD_INNER + SSD_CONV_CH + SSD_HEADS
LRU_WIDTH = MIX_WIDTH - SSD_INNER
LRU_BLOCKS = 8
LRU_BD = LRU_WIDTH // LRU_BLOCKS
LRU_C = 8.0
CONV_W = 4
PEER_HEADS = 8
N_KEYS = 128
PEER_TOPK = 16
PEER_DQ = 256
PEER_BLOCK = 128

VMEM_LIMIT_BYTES = 56 * 1024 * 1024


def _mm_kernel(a_ref, b_ref, o_ref):
    o_ref[...] = jnp.dot(a_ref[...].astype(jnp.bfloat16), b_ref[...].astype(jnp.bfloat16),
                         preferred_element_type=jnp.float32)


def pmm(a, b, tm=256):
    M, K = a.shape
    N = b.shape[1]
    assert M % tm == 0
    return pl.pallas_call(
        _mm_kernel,
        grid=(M // tm,),
        in_specs=[pl.BlockSpec((tm, K), lambda i: (i, 0)), pl.BlockSpec((K, N), lambda i: (0, 0))],
        out_specs=pl.BlockSpec((tm, N), lambda i: (i, 0)),
        out_shape=jax.ShapeDtypeStruct((M, N), jnp.float32),
        compiler_params=pltpu.CompilerParams(dimension_semantics=("arbitrary",),
                                             vmem_limit_bytes=VMEM_LIMIT_BYTES),
        name="pmm",
    )(a, b)


def mm3(x, w):
    B, T, K = x.shape
    return pmm(x.reshape(B * T, K), w).reshape(B, T, w.shape[1])


def rmsnorm(x, g, eps=1e-6):
    xf = x.astype(jnp.float32)
    y = xf * lax.rsqrt(jnp.mean(xf * xf, -1, keepdims=True) + eps)
    return (y * g.astype(jnp.float32)).astype(x.dtype)


def modulate(x, g, shift, scale):
    return rmsnorm(x, g) * (1 + scale) + shift


def split_cols(p, sizes):
    return jnp.split(p, [int(i) for i in np.cumsum(sizes)], axis=-1)


def flip(t):
    return jnp.flip(t, axis=1)


def grid_to_cols(x):
    B, T, C = x.shape
    rows = T // GRID_W
    return x.reshape(B, rows, GRID_W, C).transpose(0, 2, 1, 3).reshape(B, T, C)


def cols_to_grid(x):
    B, T, C = x.shape
    rows = T // GRID_W
    return x.reshape(B, GRID_W, rows, C).transpose(0, 2, 1, 3).reshape(B, T, C)


def qshift_grid(x):
    B, T, C = x.shape
    rows = T // GRID_W
    g = jnp.pad(x.reshape(B, rows, GRID_W, C), ((0, 0), (1, 1), (1, 1), (0, 0)))
    sel = jnp.arange(C) % 4
    out = jnp.where(sel == 0, g[:, 1:-1, :-2], jnp.where(sel == 1, g[:, 1:-1, 2:],
                    jnp.where(sel == 2, g[:, :-2, 1:-1], g[:, 2:, 1:-1])))
    return out.reshape(B, T, C)


def shift_seq(x):
    prev = jnp.pad(x, ((0, 0), (1, 0), (0, 0)))[:, :-1]
    nxt = jnp.pad(x, ((0, 0), (0, 1), (0, 0)))[:, 1:]
    return jnp.where(jnp.arange(x.shape[-1]) % 2 == 0, prev, nxt)


def dwconv(x, w, b):
    y = lax.conv_general_dilated(x, w[:, None, :].astype(x.dtype), (1,), [((CONV_W - 1) // 2, CONV_W // 2)],
                                 dimension_numbers=("NWC", "WIO", "NWC"), feature_group_count=x.shape[-1])
    return y + b


def rwkv7_scan(r, w, k, v, kk, a, S0):
    xs = tuple(jnp.moveaxis(t.astype(jnp.float32), 1, 0) for t in (r, w, k, v, kk, a))

    def step(S, inp):
        r_t, w_t, k_t, v_t, kk_t, a_t = inp
        sk = jnp.einsum("bhvk,bhk->bhv", S, kk_t)
        S = S * w_t[:, :, None, :] - sk[..., None] * (kk_t * a_t)[:, :, None, :] + v_t[..., None] * k_t[:, :, None, :]
        return S, jnp.einsum("bhvk,bhk->bhv", S, r_t)

    S, ys = lax.scan(step, S0, xs)
    return jnp.moveaxis(ys, 0, 1).astype(r.dtype), S


def rwkv_stream(p, S0f, S0b, w0, w_up, a0, a_up, g_up, k_k, k_a, r_k, ln_g, ln_b):
    B, T, _ = p.shape
    r, k, v, wd, ad, gd = split_cols(p, [RW_WIDTH, RW_WIDTH, RW_WIDTH, RW_DECAY_RANK, RW_A_RANK])
    heads = lambda t: t.reshape(B, T, RW_HEADS, RW_HD)
    kk = heads(k * k_k).astype(jnp.float32)
    kk = kk * lax.rsqrt(jnp.maximum(jnp.sum(kk * kk, -1, keepdims=True), 1e-12))
    g = jax.nn.sigmoid(gd) @ g_up
    ys, states, bonus = [], [], []
    for d, S0 in enumerate((S0f, S0b)):
        wl = (w0[d] + jnp.tanh(wd) @ w_up[d]).astype(jnp.float32)
        decay = jnp.exp(-jnp.exp(-jax.nn.softplus(-wl) - 0.5))
        a = jax.nn.sigmoid(a0[d] + ad @ a_up[d])
        kd = k * (1 + (a - 1) * k_a)
        args = [heads(r), heads(decay), heads(kd), heads(v), kk, heads(a)]
        if d == 1:
            args = [flip(t) for t in args]
        yd, Sd = rwkv7_scan(*args, S0)
        ys.append(flip(yd) if d == 1 else yd)
        states.append(Sd)
        bonus.append(jnp.sum(heads(r) * heads(kd) * r_k, -1, keepdims=True) * heads(v))
    y = (ys[0] + ys[1]).astype(jnp.float32)
    mu = jnp.mean(y, -1, keepdims=True)
    var = jnp.mean(jnp.square(y - mu), -1, keepdims=True)
    yn = ((y - mu) * lax.rsqrt(var + RW_GN_EPS)).reshape(B, T, RW_WIDTH) * ln_g + ln_b
    out = (yn + (bonus[0] + bonus[1]).reshape(B, T, RW_WIDTH)) * g
    return out.astype(p.dtype), states[0], states[1]


def gla_chunked(q, k, v, g, S0):
    B, T, H, K = q.shape
    V = v.shape[-1]
    L = GLA_CHUNK
    n = T // L
    chunks = lambda t: t.astype(jnp.float32).reshape(B, n, L, H, t.shape[-1]).transpose(1, 0, 3, 2, 4)
    mask = jnp.tril(jnp.ones((L, L), bool))

    def step(S, inp):
        qc, kc, vc, gc = inp
        b = jnp.cumsum(gc, axis=2)
        diff = b[:, :, :, None, :] - b[:, :, None, :, :]
        dec = jnp.exp(jnp.where(mask[:, :, None], diff, -jnp.inf))
        A = jnp.einsum("bhik,bhjk,bhijk->bhij", qc, kc, dec)
        o = A @ vc + jnp.einsum("bhik,bhkv->bhiv", qc * jnp.exp(b), S)
        b_last = b[:, :, -1:, :]
        S = jnp.exp(b_last[:, :, 0, :])[..., None] * S + jnp.einsum("bhjk,bhjv->bhkv", kc * jnp.exp(b_last - b), vc)
        return S, o

    S, o = lax.scan(step, S0, (chunks(q), chunks(k), chunks(v), chunks(g)))
    return o.transpose(1, 0, 3, 2, 4).reshape(B, T, H, V).astype(q.dtype), S


def gla_stream(p, S0f, S0b, gate_up, gate_b, norm_g):
    B, T, _ = p.shape
    q, k, v, gd, og = split_cols(p, [GLA_HEADS * GLA_DK, GLA_HEADS * GLA_DK, GLA_HEADS * GLA_DV, GLA_GATE_RANK])
    q = q.reshape(B, T, GLA_HEADS, GLA_DK) * GLA_DK ** -0.5
    k = k.reshape(B, T, GLA_HEADS, GLA_DK)
    v = v.reshape(B, T, GLA_HEADS, GLA_DV)
    outs, states = [], []
    for d, S0 in enumerate((S0f, S0b)):
        glog = jax.nn.log_sigmoid((gd @ gate_up[d] + gate_b[d]).astype(jnp.float32)) / GLA_GATE_NORM
        args = [q, k, v, glog.reshape(B, T, GLA_HEADS, GLA_DK)]
        if d == 1:
            args = [flip(t) for t in args]
        od, Sd = gla_chunked(*args, S0)
        outs.append(flip(od) if d == 1 else od)
        states.append(Sd)
    o = rmsnorm(outs[0] + outs[1], norm_g.reshape(GLA_HEADS, GLA_DV)).reshape(B, T, GLA_HEADS * GLA_DV)
    return (o * jax.nn.silu(og)).astype(p.dtype), states[0], states[1]


def ssd_chunked(x, dt, A, Bm, Cm, S0):
    Bsz, T, H, P = x.shape
    N = Bm.shape[-1]
    L = SSD_CHUNK
    n = T // L
    f = lambda t: t.astype(jnp.float32)
    xc = (f(x) * f(dt)[..., None]).reshape(Bsz, n, L, H, P)
    Bc = f(Bm).reshape(Bsz, n, L, H, N)
    Cc = f(Cm).reshape(Bsz, n, L, H, N)
    acs = jnp.cumsum((f(dt) * A).reshape(Bsz, n, L, H).transpose(0, 3, 1, 2), -1)
    mask = jnp.tril(jnp.ones((L, L), bool))
    Lmat = jnp.exp(jnp.where(mask, acs[..., :, None] - acs[..., None, :], -jnp.inf))
    G = jnp.einsum("bclhn,bcshn->bhcls", Cc, Bc) * Lmat
    y_diag = jnp.einsum("bhcls,bcshp->bclhp", G, xc)
    contrib = jnp.einsum("bclhn,bhcl,bclhp->bchpn", Bc, jnp.exp(acs[..., -1:] - acs), xc)
    chunk_decay = jnp.exp(acs[..., -1])

    def step(S, inp):
        dec, st = inp
        return dec[..., None, None] * S + st, S

    S, S_in = lax.scan(step, S0, (jnp.moveaxis(chunk_decay, 2, 0), jnp.moveaxis(contrib, 1, 0)))
    y_off = jnp.einsum("bclhn,bchpn,bhcl->bclhp", Cc, jnp.moveaxis(S_in, 0, 1), jnp.exp(acs))
    return (y_diag + y_off).reshape(Bsz, T, H, P).astype(x.dtype), S


def ssd_stream(p, S0f, S0b, conv_w, conv_b, dt_bias, A_log, D_skip, norm_g):
    B, T, _ = p.shape
    z, xbc, dtr = split_cols(p, [SSD_INNER, SSD_CONV_CH])
    xbc = jax.nn.silu(dwconv(xbc, conv_w, conv_b))
    xs, Bm, Cm = split_cols(xbc, [SSD_INNER, SSD_GROUPS * SSD_N])
    xs = xs.reshape(B, T, SSD_HEADS, SSD_P)
    rep = SSD_HEADS // SSD_GROUPS
    Bm = jnp.repeat(Bm.reshape(B, T, SSD_GROUPS, SSD_N), rep, axis=2)
    Cm = jnp.repeat(Cm.reshape(B, T, SSD_GROUPS, SSD_N), rep, axis=2)
    ys, states = [], []
    for d, S0 in enumerate((S0f, S0b)):
        dt = jax.nn.softplus((dtr + dt_bias[d]).astype(jnp.float32))
        A = -jnp.exp(A_log[d].astype(jnp.float32))
        args = [xs, dt, Bm, Cm]
        if d == 1:
            args = [flip(t) for t in args]
        yd, Sd = ssd_chunked(args[0], args[1], A, args[2], args[3], S0)
        ys.append(flip(yd) if d == 1 else yd)
        states.append(Sd)
    y = (ys[0] + ys[1] + xs * D_skip[:, None]).reshape(B, T, SSD_INNER) * jax.nn.silu(z)
    return rmsnorm(y, norm_g).astype(p.dtype), states[0], states[1]


def linear_scan(a, b, h0):
    b = b.at[:, 0].add(a[:, 0] * h0)
    comb = lambda l, r: (l[0] * r[0], r[0] * l[1] + r[1])
    _, h = lax.associative_scan(comb, (a, b), axis=1)
    return h, h[:, -1]


def lru_stream(p, h0f, h0b, conv_w, conv_b, wa, ba, wx, bx, lam):
    B, T, _ = p.shape
    xr, gate = split_cols(p, [LRU_WIDTH])
    xf = dwconv(xr, conv_w, conv_b).astype(jnp.float32)
    xh = xf.reshape(B, T, LRU_BLOCKS, LRU_BD)
    hs, finals = [], []
    for d, h0 in enumerate((h0f, h0b)):
        r = jax.nn.sigmoid(jnp.einsum("btgi,gij->btgj", xh, wa[d]).reshape(B, T, LRU_WIDTH) + ba[d])
        i = jax.nn.sigmoid(jnp.einsum("btgi,gij->btgj", xh, wx[d]).reshape(B, T, LRU_WIDTH) + bx[d])
        log_a = -LRU_C * r * jax.nn.softplus(-lam[d])
        a = jnp.exp(log_a)
        b = jnp.sqrt(-jnp.expm1(2 * log_a)) * (i * xf)
        if d == 1:
            a, b = flip(a), flip(b)
        h, hT = linear_scan(a, b, h0)
        hs.append(flip(h) if d == 1 else h)
        finals.append(hT)
    y = (hs[0] + hs[1]).astype(p.dtype) * jax.nn.gelu(gate)
    return y, finals[0], finals[1]


def even_mixer(u, uc, w_in, mu, rw_p, gla_p):
    B = u.shape[0]
    p, pc = mm3(u, w_in), mm3(uc, w_in)
    rw, gl = p[..., :RW_COLS], p[..., RW_COLS:]
    rwc, glc = pc[..., :RW_COLS], pc[..., RW_COLS:]
    rw = rw + mu * (qshift_grid(rw) - rw)
    rwc = rwc + mu * (shift_seq(rwc) - rwc)
    z_rw = jnp.zeros((B, RW_HEADS, RW_HD, RW_HD), jnp.float32)
    yc_rw, Sf, Sb = rwkv_stream(rwc, z_rw, z_rw, *rw_p)
    y_rw, _, _ = rwkv_stream(rw, Sf, Sb, *rw_p)
    z_gla = jnp.zeros((B, GLA_HEADS, GLA_DK, GLA_DV), jnp.float32)
    yc_gl, Gf, Gb = gla_stream(glc, z_gla, z_gla, *gla_p)
    y_gl, _, _ = gla_stream(gl, Gf, Gb, *gla_p)
    return jnp.concatenate([y_rw, y_gl], -1), jnp.concatenate([yc_rw, yc_gl], -1)


def odd_mixer(u, uc, w_in, ssd_p, lru_p):
    B = u.shape[0]
    p = grid_to_cols(mm3(u, w_in))
    pc = mm3(uc, w_in)
    z_s = jnp.zeros((B, SSD_HEADS, SSD_P, SSD_N), jnp.float32)
    yc_s, Sf, Sb = ssd_stream(pc[..., :SSD_COLS], z_s, z_s, *ssd_p)
    y_s, _, _ = ssd_stream(p[..., :SSD_COLS], Sf, Sb, *ssd_p)
    z_l = jnp.zeros((B, LRU_WIDTH), jnp.float32)
    yc_l, hf, hb = lru_stream(pc[..., SSD_COLS:], z_l, z_l, *lru_p)
    y_l, _, _ = lru_stream(p[..., SSD_COLS:], hf, hb, *lru_p)
    return cols_to_grid(jnp.concatenate([y_s, y_l], -1)), jnp.concatenate([yc_s, yc_l], -1)


def peer(x, wq, keys, u_tab, v_tab):
    B, T, D = x.shape
    M = B * T
    xf = x.reshape(M, D)
    q = pmm(xf, wq).astype(jnp.float32).reshape(M, PEER_HEADS, 2, PEER_DQ // 2)
    s = jnp.einsum("mhsd,hskd->mhsk", q, keys.astype(jnp.float32))
    top_s, top_i = lax.top_k(s, PEER_TOPK)
    cand_s = (top_s[:, :, 0, :, None] + top_s[:, :, 1, None, :]).reshape(M, PEER_HEADS, PEER_TOPK * PEER_TOPK)
    cand_i = (top_i[:, :, 0, :, None] * N_KEYS + top_i[:, :, 1, None, :]).reshape(M, PEER_HEADS, PEER_TOPK * PEER_TOPK)
    best_s, pos = lax.top_k(cand_s, PEER_TOPK)
    idx = jnp.take_along_axis(cand_i, pos, axis=-1)
    gate = jax.nn.softmax(best_s, axis=-1).astype(x.dtype)
    nb = M // PEER_BLOCK
    E = PEER_HEADS * PEER_TOPK

    def block(args):
        xs, ids, gs = args
        act = jax.nn.gelu(jnp.einsum("md,med->me", xs, u_tab[ids]))
        return jnp.einsum("me,med->md", act * gs, v_tab[ids])

    out = lax.map(block, (xf.reshape(nb, PEER_BLOCK, D), idx.reshape(nb, PEER_BLOCK, E), gate.reshape(nb, PEER_BLOCK, E)))
    return out.reshape(B, T, D)


def kernel(x, c, ctx, c_ctx, ada_w, ada_b, norm1_g, norm2_g, ev_w_in, rw_mu, rw_w0, rw_w_up, rw_a0, rw_a_up,
           rw_g_up, rw_k_k, rw_k_a, rw_r_k, rw_ln_g, rw_ln_b, gla_gate_up, gla_gate_b, gla_norm_g, od_w_in,
           ssd_conv_w, ssd_conv_b, ssd_dt_bias, ssd_A_log, ssd_D, ssd_norm_g, lru_conv_w, lru_conv_b, lru_wa,
           lru_ba, lru_wx, lru_bx, lru_lam, w_out, peer_wq, peer_keys, peer_u, peer_v, final_g):
    h, hc = x, ctx
    sc, scc = jax.nn.silu(c), jax.nn.silu(c_ctx)
    for l in range(DEPTH):
        last = l == DEPTH - 1
        i = l // 2
        mod = jnp.split((sc @ ada_w[l] + ada_b[l])[:, None, :], 6, axis=-1)
        modc = jnp.split(scc @ ada_w[l] + ada_b[l], 6, axis=-1)
        u = modulate(h, norm1_g[l], mod[0], mod[1])
        uc = modulate(hc, norm1_g[l], modc[0], modc[1])
        if l % 2 == 0:
            rw_p = (rw_w0[i], rw_w_up[i], rw_a0[i], rw_a_up[i], rw_g_up[i], rw_k_k[i], rw_k_a[i], rw_r_k[i],
                    rw_ln_g[i], rw_ln_b[i])
            gla_p = (gla_gate_up[i], gla_gate_b[i], gla_norm_g[i])
            y, yc = even_mixer(u, uc, ev_w_in[i], rw_mu[i], rw_p, gla_p)
        else:
            ssd_p = (ssd_conv_w[i], ssd_conv_b[i], ssd_dt_bias[i], ssd_A_log[i], ssd_D[i], ssd_norm_g[i])
            lru_p = (lru_conv_w[i], lru_conv_b[i], lru_wa[i], lru_ba[i], lru_wx[i], lru_bx[i], lru_lam[i])
            y, yc = odd_mixer(u, uc, od_w_in[i], ssd_p, lru_p)
        h = h + mod[2] * mm3(y, w_out[l])
        u = modulate(h, norm2_g[l], mod[3], mod[4])
        h = h + mod[5] * peer(u, peer_wq[l], peer_keys[l], peer_u[l], peer_v[l])
        if not last:
            hc = hc + modc[2] * mm3(yc, w_out[l])
            uc = modulate(hc, norm2_g[l], modc[3], modc[4])
            hc = hc + modc[5] * peer(uc, peer_wq[l], peer_keys[l], peer_u[l], peer_v[l])
    return rmsnorm(h, final_g)
```

```python
import functools

import jax
import jax.numpy as jnp
from jax import lax
import numpy as np
from jax.experimental import pallas as pl
from jax.experimental.pallas import tpu as pltpu

D_MODEL = 1024
DEPTH = 4
GRID_W = 64
MIX_WIDTH = D_MODEL

RW_WIDTH = MIX_WIDTH // 2
RW_HD = 64
RW_HEADS = RW_WIDTH // RW_HD
RW_DECAY_RANK = 32
RW_A_RANK = 32
RW_GATE_RANK = 64
RW_GN_EPS = 64e-5
RW_COLS = 3 * RW_WIDTH + RW_DECAY_RANK + RW_A_RANK + RW_GATE_RANK
GLA_HEADS = 4
GLA_DV = (MIX_WIDTH - RW_WIDTH) // GLA_HEADS
GLA_DK = GLA_DV // 2
GLA_GATE_RANK = 16
GLA_GATE_NORM = 16.0
GLA_CHUNK = 64
SSD_INNER = MIX_WIDTH // 2
SSD_P = 64
SSD_HEADS = SSD_INNER // SSD_P
SSD_N = 128
SSD_GROUPS = 2
SSD_CHUNK = 128
SSD_CONV_CH = SSD_INNER + 2 * SSD_GROUPS * SSD_N
SSD_COLS = SSD_INNER + SSD_CONV_CH + SSD_HEADS
LRU_WIDTH = MIX_WIDTH - SSD_INNER
LRU_BLOCKS = 8
LRU_BD = LRU_WIDTH // LRU_BLOCKS
LRU_C = 8.0
CONV_W = 4
PEER_HEADS = 8
N_KEYS = 128
PEER_TOPK = 16
PEER_DQ = 256
PEER_BLOCK = 128

VMEM_LIMIT_BYTES = 56 * 1024 * 1024


def _mm_kernel(a_ref, b_ref, o_ref):
    o_ref[...] = jnp.dot(a_ref[...].astype(jnp.bfloat16), b_ref[...].astype(jnp.bfloat16),
                         preferred_element_type=jnp.float32)


def pmm(a, b, tm=256):
    M, K = a.shape
    N = b.shape[1]
    assert M % tm == 0
    return pl.pallas_call(
        _mm_kernel,
        grid=(M // tm,),
        in_specs=[pl.BlockSpec((tm, K), lambda i: (i, 0)), pl.BlockSpec((K, N), lambda i: (0, 0))],
        out_specs=pl.BlockSpec((tm, N), lambda i: (i, 0)),
        out_shape=jax.ShapeDtypeStruct((M, N), jnp.float32),
        compiler_params=pltpu.CompilerParams(dimension_semantics=("arbitrary",),
                                             vmem_limit_bytes=VMEM_LIMIT_BYTES),
        name="pmm",
    )(a, b)


SUBLANES = 8
LANES = 128
PEER_E = PEER_HEADS * PEER_TOPK
PEER_TB = 8


def _fold8(p):
    sub = lax.broadcasted_iota(jnp.int32, (SUBLANES, LANES), 0)

    def comb(a, b, k):
        fa = a + pltpu.roll(a, SUBLANES - k, 0)
        fb = b + pltpu.roll(b, k, 0)
        return jnp.where((sub & k) == 0, fa, fb)

    return comb(comb(comb(p[0], p[4], 4), comb(p[2], p[6], 4), 2),
                comb(comb(p[1], p[5], 4), comb(p[3], p[7], 4), 2), 1)


def _peer_kernel(idx_cur_ref, idx_nxt_ref, gate_ref, x_ref, tab_ref, o_ref, buf, sem, cs_ref):
    i = pl.program_id(0)
    n = pl.num_programs(0)
    slot = i % 2
    nslot = 1 - slot

    def issue(idx_ref, t, s):
        for e in range(PEER_E):
            j = t * PEER_E + e
            pltpu.make_async_copy(tab_ref.at[idx_ref[0, 0, j]], buf.at[s, j], sem.at[s]).start()

    def wait_slot(s):
        pltpu.make_async_copy(buf.at[s], buf.at[s], sem.at[s]).wait()

    @pl.when(i == 0)
    def _():
        lax.fori_loop(0, PEER_TB, lambda t, c: (issue(idx_cur_ref, t, 0), c)[1], 0)

    wait_slot(slot)

    def token(t, carry):
        issue(idx_nxt_ref, t, nslot)
        xt = x_ref[t]
        base = t * PEER_E
        for g in range(PEER_E // SUBLANES):
            s_g = _fold8([buf[slot, base + g * SUBLANES + j, 0] * xt for j in range(SUBLANES)])
            a = jnp.sum(s_g, axis=1, keepdims=True)
            cs_ref[pl.ds(g * SUBLANES, SUBLANES), :] = jnp.broadcast_to(jax.nn.gelu(a), (SUBLANES, LANES))
        acc = jnp.zeros((SUBLANES, LANES), jnp.float32)
        for e in range(PEER_E):
            c = cs_ref[pl.ds(e, 1), :] * gate_ref[0, 0, base + e]
            acc = acc + c * buf[slot, base + e, 1]
        o_ref[t] = acc
        return carry

    lax.fori_loop(0, PEER_TB, token, 0)

    @pl.when(i == n - 1)
    def _():
        wait_slot(nslot)


def peer_experts(xf, idx, gate, u_tab, v_tab):
    M, D = xf.shape
    NE = u_tab.shape[0]
    assert D == SUBLANES * LANES and M % PEER_TB == 0 and idx.shape == (M, PEER_E)
    nblk = M // PEER_TB
    tab = jnp.stack([u_tab.reshape(NE, SUBLANES, LANES), v_tab.reshape(NE, SUBLANES, LANES)], axis=1)
    idx3 = idx.astype(jnp.int32).reshape(nblk, 1, PEER_TB * PEER_E)
    gate3 = gate.astype(jnp.float32).reshape(nblk, 1, PEER_TB * PEER_E)
    x3 = xf.reshape(M, SUBLANES, LANES)
    smem_blk = (1, 1, PEER_TB * PEER_E)
    out = pl.pallas_call(
        _peer_kernel,
        grid=(nblk,),
        in_specs=[
            pl.BlockSpec(smem_blk, lambda i: (i, 0, 0), memory_space=pltpu.SMEM),
            pl.BlockSpec(smem_blk, lambda i: (jnp.minimum(i + 1, nblk - 1), 0, 0), memory_space=pltpu.SMEM),
            pl.BlockSpec(smem_blk, lambda i: (i, 0, 0), memory_space=pltpu.SMEM),
            pl.BlockSpec((PEER_TB, SUBLANES, LANES), lambda i: (i, 0, 0)),
            pl.BlockSpec(memory_space=pl.ANY),
        ],
        out_specs=pl.BlockSpec((PEER_TB, SUBLANES, LANES), lambda i: (i, 0, 0)),
        out_shape=jax.ShapeDtypeStruct((M, SUBLANES, LANES), jnp.float32),
        scratch_shapes=[
            pltpu.VMEM((2, PEER_TB * PEER_E, 2, SUBLANES, LANES), jnp.float32),
            pltpu.SemaphoreType.DMA((2,)),
            pltpu.VMEM((PEER_E, LANES), jnp.float32),
        ],
        compiler_params=pltpu.CompilerParams(dimension_semantics=("arbitrary",),
                                             vmem_limit_bytes=VMEM_LIMIT_BYTES),
        name="peer_experts",
    )(idx3, idx3, gate3, x3, tab)
    return out.reshape(M, D)


def mm3(x, w):
    B, T, K = x.shape
    return pmm(x.reshape(B * T, K), w).reshape(B, T, w.shape[1])


def rmsnorm(x, g, eps=1e-6):
    xf = x.astype(jnp.float32)
    y = xf * lax.rsqrt(jnp.mean(xf * xf, -1, keepdims=True) + eps)
    return (y * g.astype(jnp.float32)).astype(x.dtype)


def modulate(x, g, shift, scale):
    return rmsnorm(x, g) * (1 + scale) + shift


def split_cols(p, sizes):
    return jnp.split(p, [int(i) for i in np.cumsum(sizes)], axis=-1)


def flip(t):
    return jnp.flip(t, axis=1)


def grid_to_cols(x):
    B, T, C = x.shape
    rows = T // GRID_W
    return x.reshape(B, rows, GRID_W, C).transpose(0, 2, 1, 3).reshape(B, T, C)


def cols_to_grid(x):
    B, T, C = x.shape
    rows = T // GRID_W
    return x.reshape(B, GRID_W, rows, C).transpose(0, 2, 1, 3).reshape(B, T, C)


def qshift_grid(x):
    B, T, C = x.shape
    rows = T // GRID_W
    g = jnp.pad(x.reshape(B, rows, GRID_W, C), ((0, 0), (1, 1), (1, 1), (0, 0)))
    sel = jnp.arange(C) % 4
    out = jnp.where(sel == 0, g[:, 1:-1, :-2], jnp.where(sel == 1, g[:, 1:-1, 2:],
                    jnp.where(sel == 2, g[:, :-2, 1:-1], g[:, 2:, 1:-1])))
    return out.reshape(B, T, C)


def shift_seq(x):
    prev = jnp.pad(x, ((0, 0), (1, 0), (0, 0)))[:, :-1]
    nxt = jnp.pad(x, ((0, 0), (0, 1), (0, 0)))[:, 1:]
    return jnp.where(jnp.arange(x.shape[-1]) % 2 == 0, prev, nxt)


def dwconv(x, w, b):
    lo, hi = (CONV_W - 1) // 2, CONV_W // 2
    T = x.shape[1]
    xp = jnp.pad(x, ((0, 0), (lo, hi), (0, 0)))
    y = sum(xp[:, j:j + T] * w[j].astype(x.dtype) for j in range(CONV_W))
    return y + b


def rwkv7_scan(r, w, k, v, kk, a, S0):
    xs = tuple(jnp.moveaxis(t.astype(jnp.float32), 1, 0) for t in (r, w, k, v, kk, a))

    def step(S, inp):
        r_t, w_t, k_t, v_t, kk_t, a_t = inp
        sk = jnp.einsum("bhvk,bhk->bhv", S, kk_t)
        S = S * w_t[:, :, None, :] - sk[..., None] * (kk_t * a_t)[:, :, None, :] + v_t[..., None] * k_t[:, :, None, :]
        return S, jnp.einsum("bhvk,bhk->bhv", S, r_t)

    S, ys = lax.scan(step, S0, xs)
    return jnp.moveaxis(ys, 0, 1).astype(r.dtype), S


def rwkv_stream(p, S0f, S0b, w0, w_up, a0, a_up, g_up, k_k, k_a, r_k, ln_g, ln_b):
    B, T, _ = p.shape
    r, k, v, wd, ad, gd = split_cols(p, [RW_WIDTH, RW_WIDTH, RW_WIDTH, RW_DECAY_RANK, RW_A_RANK])
    heads = lambda t: t.reshape(B, T, RW_HEADS, RW_HD)
    kk = heads(k * k_k).astype(jnp.float32)
    kk = kk * lax.rsqrt(jnp.maximum(jnp.sum(kk * kk, -1, keepdims=True), 1e-12))
    g = jax.nn.sigmoid(gd) @ g_up
    ys, states, bonus = [], [], []
    for d, S0 in enumerate((S0f, S0b)):
        wl = (w0[d] + jnp.tanh(wd) @ w_up[d]).astype(jnp.float32)
        decay = jnp.exp(-jnp.exp(-jax.nn.softplus(-wl) - 0.5))
        a = jax.nn.sigmoid(a0[d] + ad @ a_up[d])
        kd = k * (1 + (a - 1) * k_a)
        args = [heads(r), heads(decay), heads(kd), heads(v), kk, heads(a)]
        if d == 1:
            args = [flip(t) for t in args]
        yd, Sd = rwkv7_scan(*args, S0)
        ys.append(flip(yd) if d == 1 else yd)
        states.append(Sd)
        bonus.append(jnp.sum(heads(r) * heads(kd) * r_k, -1, keepdims=True) * heads(v))
    y = (ys[0] + ys[1]).astype(jnp.float32)
    mu = jnp.mean(y, -1, keepdims=True)
    var = jnp.mean(jnp.square(y - mu), -1, keepdims=True)
    yn = ((y - mu) * lax.rsqrt(var + RW_GN_EPS)).reshape(B, T, RW_WIDTH) * ln_g + ln_b
    out = (yn + (bonus[0] + bonus[1]).reshape(B, T, RW_WIDTH)) * g
    return out.astype(p.dtype), states[0], states[1]


def gla_chunked(q, k, v, g, S0):
    B, T, H, K = q.shape
    V = v.shape[-1]
    L = GLA_CHUNK
    n = T // L
    chunks = lambda t: t.astype(jnp.float32).reshape(B, n, L, H, t.shape[-1]).transpose(1, 0, 3, 2, 4)
    mask = jnp.tril(jnp.ones((L, L), bool))

    def step(S, inp):
        qc, kc, vc, gc = inp
        b = jnp.cumsum(gc, axis=2)
        diff = b[:, :, :, None, :] - b[:, :, None, :, :]
        dec = jnp.exp(jnp.where(mask[:, :, None], diff, -jnp.inf))
        A = jnp.einsum("bhik,bhjk,bhijk->bhij", qc, kc, dec)
        o = A @ vc + jnp.einsum("bhik,bhkv->bhiv", qc * jnp.exp(b), S)
        b_last = b[:, :, -1:, :]
        S = jnp.exp(b_last[:, :, 0, :])[..., None] * S + jnp.einsum("bhjk,bhjv->bhkv", kc * jnp.exp(b_last - b), vc)
        return S, o

    S, o = lax.scan(step, S0, (chunks(q), chunks(k), chunks(v), chunks(g)))
    return o.transpose(1, 0, 3, 2, 4).reshape(B, T, H, V).astype(q.dtype), S


def gla_stream(p, S0f, S0b, gate_up, gate_b, norm_g):
    B, T, _ = p.shape
    q, k, v, gd, og = split_cols(p, [GLA_HEADS * GLA_DK, GLA_HEADS * GLA_DK, GLA_HEADS * GLA_DV, GLA_GATE_RANK])
    q = q.reshape(B, T, GLA_HEADS, GLA_DK) * GLA_DK ** -0.5
    k = k.reshape(B, T, GLA_HEADS, GLA_DK)
    v = v.reshape(B, T, GLA_HEADS, GLA_DV)
    outs, states = [], []
    for d, S0 in enumerate((S0f, S0b)):
        glog = jax.nn.log_sigmoid((gd @ gate_up[d] + gate_b[d]).astype(jnp.float32)) / GLA_GATE_NORM
        args = [q, k, v, glog.reshape(B, T, GLA_HEADS, GLA_DK)]
        if d == 1:
            args = [flip(t) for t in args]
        od, Sd = gla_chunked(*args, S0)
        outs.append(flip(od) if d == 1 else od)
        states.append(Sd)
    o = rmsnorm(outs[0] + outs[1], norm_g.reshape(GLA_HEADS, GLA_DV)).reshape(B, T, GLA_HEADS * GLA_DV)
    return (o * jax.nn.silu(og)).astype(p.dtype), states[0], states[1]


def ssd_chunked(x, dt, A, Bm, Cm, S0):
    Bsz, T, H, P = x.shape
    N = Bm.shape[-1]
    L = SSD_CHUNK
    n = T // L
    f = lambda t: t.astype(jnp.float32)
    xc = (f(x) * f(dt)[..., None]).reshape(Bsz, n, L, H, P)
    Bc = f(Bm).reshape(Bsz, n, L, H, N)
    Cc = f(Cm).reshape(Bsz, n, L, H, N)
    acs = jnp.cumsum((f(dt) * A).reshape(Bsz, n, L, H).transpose(0, 3, 1, 2), -1)
    mask = jnp.tril(jnp.ones((L, L), bool))
    Lmat = jnp.exp(jnp.where(mask, acs[..., :, None] - acs[..., None, :], -jnp.inf))
    G = jnp.einsum("bclhn,bcshn->bhcls", Cc, Bc) * Lmat
    y_diag = jnp.einsum("bhcls,bcshp->bclhp", G, xc)
    contrib = jnp.einsum("bclhn,bhcl,bclhp->bchpn", Bc, jnp.exp(acs[..., -1:] - acs), xc)
    chunk_decay = jnp.exp(acs[..., -1])

    def step(S, inp):
        dec, st = inp
        return dec[..., None, None] * S + st, S

    S, S_in = lax.scan(step, S0, (jnp.moveaxis(chunk_decay, 2, 0), jnp.moveaxis(contrib, 1, 0)))
    y_off = jnp.einsum("bclhn,bchpn,bhcl->bclhp", Cc, jnp.moveaxis(S_in, 0, 1), jnp.exp(acs))
    return (y_diag + y_off).reshape(Bsz, T, H, P).astype(x.dtype), S


def ssd_stream(p, S0f, S0b, conv_w, conv_b, dt_bias, A_log, D_skip, norm_g):
    B, T, _ = p.shape
    z, xbc, dtr = split_cols(p, [SSD_INNER, SSD_CONV_CH])
    xbc = jax.nn.silu(dwconv(xbc, conv_w, conv_b))
    xs, Bm, Cm = split_cols(xbc, [SSD_INNER, SSD_GROUPS * SSD_N])
    xs = xs.reshape(B, T, SSD_HEADS, SSD_P)
    rep = SSD_HEADS // SSD_GROUPS
    Bm = jnp.repeat(Bm.reshape(B, T, SSD_GROUPS, SSD_N), rep, axis=2)
    Cm = jnp.repeat(Cm.reshape(B, T, SSD_GROUPS, SSD_N), rep, axis=2)
    ys, states = [], []
    for d, S0 in enumerate((S0f, S0b)):
        dt = jax.nn.softplus((dtr + dt_bias[d]).astype(jnp.float32))
        A = -jnp.exp(A_log[d].astype(jnp.float32))
        args = [xs, dt, Bm, Cm]
        if d == 1:
            args = [flip(t) for t in args]
        yd, Sd = ssd_chunked(args[0], args[1], A, args[2], args[3], S0)
        ys.append(flip(yd) if d == 1 else yd)
        states.append(Sd)
    y = (ys[0] + ys[1] + xs * D_skip[:, None]).reshape(B, T, SSD_INNER) * jax.nn.silu(z)
    return rmsnorm(y, norm_g).astype(p.dtype), states[0], states[1]


def linear_scan(a, b, h0):
    b = b.at[:, 0].add(a[:, 0] * h0)
    comb = lambda l, r: (l[0] * r[0], r[0] * l[1] + r[1])
    _, h = lax.associative_scan(comb, (a, b), axis=1)
    return h, h[:, -1]


def lru_stream(p, h0f, h0b, conv_w, conv_b, wa, ba, wx, bx, lam):
    B, T, _ = p.shape
    xr, gate = split_cols(p, [LRU_WIDTH])
    xf = dwconv(xr, conv_w, conv_b).astype(jnp.float32)
    xh = xf.reshape(B, T, LRU_BLOCKS, LRU_BD)
    hs, finals = [], []
    for d, h0 in enumerate((h0f, h0b)):
        r = jax.nn.sigmoid(jnp.einsum("btgi,gij->btgj", xh, wa[d]).reshape(B, T, LRU_WIDTH) + ba[d])
        i = jax.nn.sigmoid(jnp.einsum("btgi,gij->btgj", xh, wx[d]).reshape(B, T, LRU_WIDTH) + bx[d])
        log_a = -LRU_C * r * jax.nn.softplus(-lam[d])
        a = jnp.exp(log_a)
        b = jnp.sqrt(-jnp.expm1(2 * log_a)) * (i * xf)
        if d == 1:
            a, b = flip(a), flip(b)
        h, hT = linear_scan(a, b, h0)
        hs.append(flip(h) if d == 1 else h)
        finals.append(hT)
    y = (hs[0] + hs[1]).astype(p.dtype) * jax.nn.gelu(gate)
    return y, finals[0], finals[1]


def even_mixer(u, uc, w_in, mu, rw_p, gla_p):
    B = u.shape[0]
    p, pc = mm3(u, w_in), mm3(uc, w_in)
    rw, gl = p[..., :RW_COLS], p[..., RW_COLS:]
    rwc, glc = pc[..., :RW_COLS], pc[..., RW_COLS:]
    rw = rw + mu * (qshift_grid(rw) - rw)
    rwc = rwc + mu * (shift_seq(rwc) - rwc)
    z_rw = jnp.zeros((B, RW_HEADS, RW_HD, RW_HD), jnp.float32)
    yc_rw, Sf, Sb = rwkv_stream(rwc, z_rw, z_rw, *rw_p)
    y_rw, _, _ = rwkv_stream(rw, Sf, Sb, *rw_p)
    z_gla = jnp.zeros((B, GLA_HEADS, GLA_DK, GLA_DV), jnp.float32)
    yc_gl, Gf, Gb = gla_stream(glc, z_gla, z_gla, *gla_p)
    y_gl, _, _ = gla_stream(gl, Gf, Gb, *gla_p)
    return jnp.concatenate([y_rw, y_gl], -1), jnp.concatenate([yc_rw, yc_gl], -1)


def odd_mixer(u, uc, w_in, ssd_p, lru_p):
    B = u.shape[0]
    p = grid_to_cols(mm3(u, w_in))
    pc = mm3(uc, w_in)
    z_s = jnp.zeros((B, SSD_HEADS, SSD_P, SSD_N), jnp.float32)
    yc_s, Sf, Sb = ssd_stream(pc[..., :SSD_COLS], z_s, z_s, *ssd_p)
    y_s, _, _ = ssd_stream(p[..., :SSD_COLS], Sf, Sb, *ssd_p)
    z_l = jnp.zeros((B, LRU_WIDTH), jnp.float32)
    yc_l, hf, hb = lru_stream(pc[..., SSD_COLS:], z_l, z_l, *lru_p)
    y_l, _, _ = lru_stream(p[..., SSD_COLS:], hf, hb, *lru_p)
    return cols_to_grid(jnp.concatenate([y_s, y_l], -1)), jnp.concatenate([yc_s, yc_l], -1)


def peer(x, wq, keys, u_tab, v_tab):
    B, T, D = x.shape
    M = B * T
    xf = x.reshape(M, D)
    q = pmm(xf, wq).astype(jnp.float32).reshape(M, PEER_HEADS, 2, PEER_DQ // 2)
    s = jnp.einsum("mhsd,hskd->mhsk", q, keys.astype(jnp.float32))
    top_s, top_i = lax.top_k(s, PEER_TOPK)
    cand_s = (top_s[:, :, 0, :, None] + top_s[:, :, 1, None, :]).reshape(M, PEER_HEADS, PEER_TOPK * PEER_TOPK)
    cand_i = (top_i[:, :, 0, :, None] * N_KEYS + top_i[:, :, 1, None, :]).reshape(M, PEER_HEADS, PEER_TOPK * PEER_TOPK)
    best_s, pos = lax.top_k(cand_s, PEER_TOPK)
    idx = jnp.take_along_axis(cand_i, pos, axis=-1)
    gate = jax.nn.softmax(best_s, axis=-1).astype(x.dtype)
    out = peer_experts(xf, idx.reshape(M, PEER_E), gate.reshape(M, PEER_E), u_tab, v_tab)
    return out.reshape(B, T, D)


def kernel(x, c, ctx, c_ctx, ada_w, ada_b, norm1_g, norm2_g, ev_w_in, rw_mu, rw_w0, rw_w_up, rw_a0, rw_a_up,
           rw_g_up, rw_k_k, rw_k_a, rw_r_k, rw_ln_g, rw_ln_b, gla_gate_up, gla_gate_b, gla_norm_g, od_w_in,
           ssd_conv_w, ssd_conv_b, ssd_dt_bias, ssd_A_log, ssd_D, ssd_norm_g, lru_conv_w, lru_conv_b, lru_wa,
           lru_ba, lru_wx, lru_bx, lru_lam, w_out, peer_wq, peer_keys, peer_u, peer_v, final_g):
    h, hc = x, ctx
    sc, scc = jax.nn.silu(c), jax.nn.silu(c_ctx)
    for l in range(DEPTH):
        last = l == DEPTH - 1
        i = l // 2
        mod = jnp.split((sc @ ada_w[l] + ada_b[l])[:, None, :], 6, axis=-1)
        modc = jnp.split(scc @ ada_w[l] + ada_b[l], 6, axis=-1)
        u = modulate(h, norm1_g[l], mod[0], mod[1])
        uc = modulate(hc, norm1_g[l], modc[0], modc[1])
        if l % 2 == 0:
            rw_p = (rw_w0[i], rw_w_up[i], rw_a0[i], rw_a_up[i], rw_g_up[i], rw_k_k[i], rw_k_a[i], rw_r_k[i],
                    rw_ln_g[i], rw_ln_b[i])
            gla_p = (gla_gate_up[i], gla_gate_b[i], gla_norm_g[i])
            y, yc = even_mixer(u, uc, ev_w_in[i], rw_mu[i], rw_p, gla_p)
        else:
            ssd_p = (ssd_conv_w[i], ssd_conv_b[i], ssd_dt_bias[i], ssd_A_log[i], ssd_D[i], ssd_norm_g[i])
            lru_p = (lru_conv_w[i], lru_conv_b[i], lru_wa[i], lru_ba[i], lru_wx[i], lru_bx[i], lru_lam[i])
            y, yc = odd_mixer(u, uc, od_w_in[i], ssd_p, lru_p)
        h = h + mod[2] * mm3(y, w_out[l])
        u = modulate(h, norm2_g[l], mod[3], mod[4])
        h = h + mod[5] * peer(u, peer_wq[l], peer_keys[l], peer_u[l], peer_v[l])
        if not last:
            hc = hc + modc[2] * mm3(yc, w_out[l])
            uc = modulate(hc, norm2_g[l], modc[3], modc[4])
            hc = hc + modc[5] * peer(uc, peer_wq[l], peer_keys[l], peer_u[l], peer_v[l])
    return rmsnorm(h, final_g)
```

```python
import functools

import jax
import jax.numpy as jnp
from jax import lax
import numpy as np
from jax.experimental import pallas as pl
from jax.experimental.pallas import tpu as pltpu

D_MODEL = 1024
DEPTH = 4
GRID_W = 64
MIX_WIDTH = D_MODEL

RW_WIDTH = MIX_WIDTH // 2
RW_HD = 64
RW_HEADS = RW_WIDTH // RW_HD
RW_DECAY_RANK = 32
RW_A_RANK = 32
RW_GATE_RANK = 64
RW_GN_EPS = 64e-5
RW_COLS = 3 * RW_WIDTH + RW_DECAY_RANK + RW_A_RANK + RW_GATE_RANK
GLA_HEADS = 4
GLA_DV = (MIX_WIDTH - RW_WIDTH) // GLA_HEADS
GLA_DK = GLA_DV // 2
GLA_GATE_RANK = 16
GLA_GATE_NORM = 16.0
GLA_CHUNK = 64
SSD_INNER = MIX_WIDTH // 2
SSD_P = 64
SSD_HEADS = SSD_INNER // SSD_P
SSD_N = 128
SSD_GROUPS = 2
SSD_CHUNK = 128
SSD_CONV_CH = SSD_INNER + 2 * SSD_GROUPS * SSD_N
SSD_COLS = SSD_INNER + SSD_CONV_CH + SSD_HEADS
LRU_WIDTH = MIX_WIDTH - SSD_INNER
LRU_BLOCKS = 8
LRU_BD = LRU_WIDTH // LRU_BLOCKS
LRU_C = 8.0
CONV_W = 4
PEER_HEADS = 8
N_KEYS = 128
PEER_TOPK = 16
PEER_DQ = 256
PEER_BLOCK = 128

VMEM_LIMIT_BYTES = 56 * 1024 * 1024


def _mm_kernel(a_ref, b_ref, o_ref):
    o_ref[...] = jnp.dot(a_ref[...].astype(jnp.bfloat16), b_ref[...].astype(jnp.bfloat16),
                         preferred_element_type=jnp.float32)


def pmm(a, b, tm=256):
    M, K = a.shape
    N = b.shape[1]
    assert M % tm == 0
    return pl.pallas_call(
        _mm_kernel,
        grid=(M // tm,),
        in_specs=[pl.BlockSpec((tm, K), lambda i: (i, 0)), pl.BlockSpec((K, N), lambda i: (0, 0))],
        out_specs=pl.BlockSpec((tm, N), lambda i: (i, 0)),
        out_shape=jax.ShapeDtypeStruct((M, N), jnp.float32),
        compiler_params=pltpu.CompilerParams(dimension_semantics=("arbitrary",),
                                             vmem_limit_bytes=VMEM_LIMIT_BYTES),
        name="pmm",
    )(a, b)


SUBLANES = 8
LANES = 128
PEER_E = PEER_HEADS * PEER_TOPK
PEER_TB = 8


def _fold8(p):
    sub = lax.broadcasted_iota(jnp.int32, (SUBLANES, LANES), 0)

    def comb(a, b, k):
        fa = a + pltpu.roll(a, SUBLANES - k, 0)
        fb = b + pltpu.roll(b, k, 0)
        return jnp.where((sub & k) == 0, fa, fb)

    return comb(comb(comb(p[0], p[4], 4), comb(p[2], p[6], 4), 2),
                comb(comb(p[1], p[5], 4), comb(p[3], p[7], 4), 2), 1)


def _peer_kernel(idx_cur_ref, idx_nxt_ref, gate_ref, x_ref, tab_ref, o_ref, buf, sem, cs_ref):
    i = pl.program_id(0)
    n = pl.num_programs(0)
    slot = i % 2
    nslot = 1 - slot

    def issue(idx_ref, t, s):
        for e in range(PEER_E):
            j = t * PEER_E + e
            pltpu.make_async_copy(tab_ref.at[idx_ref[0, 0, j]], buf.at[s, j], sem.at[s]).start()

    def wait_slot(s):
        pltpu.make_async_copy(buf.at[s], buf.at[s], sem.at[s]).wait()

    @pl.when(i == 0)
    def _():
        lax.fori_loop(0, PEER_TB, lambda t, c: (issue(idx_cur_ref, t, 0), c)[1], 0)

    wait_slot(slot)

    def token(t, carry):
        issue(idx_nxt_ref, t, nslot)
        xt = x_ref[t]
        base = t * PEER_E
        for g in range(PEER_E // SUBLANES):
            s_g = _fold8([buf[slot, base + g * SUBLANES + j, 0] * xt for j in range(SUBLANES)])
            a = jnp.sum(s_g, axis=1, keepdims=True)
            cs_ref[pl.ds(g * SUBLANES, SUBLANES), :] = jnp.broadcast_to(jax.nn.gelu(a), (SUBLANES, LANES))
        acc = jnp.zeros((SUBLANES, LANES), jnp.float32)
        for e in range(PEER_E):
            c = cs_ref[pl.ds(e, 1), :] * gate_ref[0, 0, base + e]
            acc = acc + c * buf[slot, base + e, 1]
        o_ref[t] = acc
        return carry

    lax.fori_loop(0, PEER_TB, token, 0)

    @pl.when(i == n - 1)
    def _():
        wait_slot(nslot)


def peer_experts(xf, idx, gate, u_tab, v_tab):
    M, D = xf.shape
    NE = u_tab.shape[0]
    assert D == SUBLANES * LANES and M % PEER_TB == 0 and idx.shape == (M, PEER_E)
    nblk = M // PEER_TB
    tab = jnp.stack([u_tab.reshape(NE, SUBLANES, LANES), v_tab.reshape(NE, SUBLANES, LANES)], axis=1)
    idx3 = idx.astype(jnp.int32).reshape(nblk, 1, PEER_TB * PEER_E)
    gate3 = gate.astype(jnp.float32).reshape(nblk, 1, PEER_TB * PEER_E)
    x3 = xf.reshape(M, SUBLANES, LANES)
    smem_blk = (1, 1, PEER_TB * PEER_E)
    out = pl.pallas_call(
        _peer_kernel,
        grid=(nblk,),
        in_specs=[
            pl.BlockSpec(smem_blk, lambda i: (i, 0, 0), memory_space=pltpu.SMEM),
            pl.BlockSpec(smem_blk, lambda i: (jnp.minimum(i + 1, nblk - 1), 0, 0), memory_space=pltpu.SMEM),
            pl.BlockSpec(smem_blk, lambda i: (i, 0, 0), memory_space=pltpu.SMEM),
            pl.BlockSpec((PEER_TB, SUBLANES, LANES), lambda i: (i, 0, 0)),
            pl.BlockSpec(memory_space=pl.ANY),
        ],
        out_specs=pl.BlockSpec((PEER_TB, SUBLANES, LANES), lambda i: (i, 0, 0)),
        out_shape=jax.ShapeDtypeStruct((M, SUBLANES, LANES), jnp.float32),
        scratch_shapes=[
            pltpu.VMEM((2, PEER_TB * PEER_E, 2, SUBLANES, LANES), jnp.float32),
            pltpu.SemaphoreType.DMA((2,)),
            pltpu.VMEM((PEER_E, LANES), jnp.float32),
        ],
        compiler_params=pltpu.CompilerParams(dimension_semantics=("arbitrary",),
                                             vmem_limit_bytes=VMEM_LIMIT_BYTES),
        name="peer_experts",
    )(idx3, idx3, gate3, x3, tab)
    return out.reshape(M, D)


ROUTE_TM = 128
PEER_HALF = PEER_DQ // 2


def _topk_rows(s, payload, k):
    n = s.shape[0]
    rid = lax.broadcasted_iota(jnp.int32, s.shape, 0)
    vals, sel = [], []
    for _ in range(k):
        m = jnp.max(s, axis=0, keepdims=True)
        pos = jnp.min(jnp.where(s == m, rid, n), axis=0, keepdims=True)
        hit = rid == pos
        vals.append(m)
        sel.append(pos if payload is None else jnp.max(jnp.where(hit, payload, -1), axis=0, keepdims=True))
        s = jnp.where(hit, -jnp.inf, s)
    return jnp.concatenate(vals, axis=0), jnp.concatenate(sel, axis=0)


def _route_kernel(x_ref, wq_ref, keys_ref, idx_ref, gate_ref):
    q = jnp.dot(x_ref[...].astype(jnp.bfloat16), wq_ref[...], preferred_element_type=jnp.float32)
    for h in range(PEER_HEADS):
        top_s, top_i = [], []
        for half in range(2):
            c0 = (h * 2 + half) * PEER_HALF
            qh = q[:, c0:c0 + PEER_HALF].astype(jnp.bfloat16)
            s = lax.dot_general(keys_ref[h, half], qh, (((1,), (1,)), ((), ())),
                                preferred_element_type=jnp.float32)
            ts, ti = _topk_rows(s, None, PEER_TOPK)
            top_s.append(ts)
            top_i.append(ti)
        cand_s = jnp.concatenate([top_s[0][i:i + 1, :] + top_s[1] for i in range(PEER_TOPK)], axis=0)
        cand_i = jnp.concatenate([top_i[0][i:i + 1, :] * N_KEYS + top_i[1] for i in range(PEER_TOPK)], axis=0)
        best_s, best_i = _topk_rows(cand_s, cand_i, PEER_TOPK)
        e = jnp.exp(best_s - best_s[0:1, :])
        r0 = h * PEER_TOPK
        gate_ref[0, r0:r0 + PEER_TOPK, :] = e / jnp.sum(e, axis=0, keepdims=True)
        idx_ref[0, r0:r0 + PEER_TOPK, :] = best_i


def peer_route(xf, wq, keys):
    M, D = xf.shape
    assert M % ROUTE_TM == 0
    nblk = M // ROUTE_TM
    out_blk = pl.BlockSpec((1, PEER_E, ROUTE_TM), lambda i: (i, 0, 0))
    idx, gate = pl.pallas_call(
        _route_kernel,
        grid=(nblk,),
        in_specs=[pl.BlockSpec((ROUTE_TM, D), lambda i: (i, 0)),
                  pl.BlockSpec(wq.shape, lambda i: (0, 0)),
                  pl.BlockSpec(keys.shape, lambda i: (0, 0, 0, 0))],
        out_specs=[out_blk, out_blk],
        out_shape=[jax.ShapeDtypeStruct((nblk, PEER_E, ROUTE_TM), jnp.int32),
                   jax.ShapeDtypeStruct((nblk, PEER_E, ROUTE_TM), jnp.float32)],
        compiler_params=pltpu.CompilerParams(dimension_semantics=("arbitrary",),
                                             vmem_limit_bytes=VMEM_LIMIT_BYTES),
        name="peer_route",
    )(xf, wq.astype(jnp.bfloat16), keys.astype(jnp.bfloat16))
    to_tokens = lambda t: t.transpose(0, 2, 1).reshape(M, PEER_E)
    return to_tokens(idx), to_tokens(gate)


def mm3(x, w):
    B, T, K = x.shape
    return pmm(x.reshape(B * T, K), w).reshape(B, T, w.shape[1])


def rmsnorm(x, g, eps=1e-6):
    xf = x.astype(jnp.float32)
    y = xf * lax.rsqrt(jnp.mean(xf * xf, -1, keepdims=True) + eps)
    return (y * g.astype(jnp.float32)).astype(x.dtype)


def modulate(x, g, shift, scale):
    return rmsnorm(x, g) * (1 + scale) + shift


def split_cols(p, sizes):
    return jnp.split(p, [int(i) for i in np.cumsum(sizes)], axis=-1)


def flip(t):
    return jnp.flip(t, axis=1)


def grid_to_cols(x):
    B, T, C = x.shape
    rows = T // GRID_W
    return x.reshape(B, rows, GRID_W, C).transpose(0, 2, 1, 3).reshape(B, T, C)


def cols_to_grid(x):
    B, T, C = x.shape
    rows = T // GRID_W
    return x.reshape(B, GRID_W, rows, C).transpose(0, 2, 1, 3).reshape(B, T, C)


def qshift_grid(x):
    B, T, C = x.shape
    rows = T // GRID_W
    g = jnp.pad(x.reshape(B, rows, GRID_W, C), ((0, 0), (1, 1), (1, 1), (0, 0)))
    sel = jnp.arange(C) % 4
    out = jnp.where(sel == 0, g[:, 1:-1, :-2], jnp.where(sel == 1, g[:, 1:-1, 2:],
                    jnp.where(sel == 2, g[:, :-2, 1:-1], g[:, 2:, 1:-1])))
    return out.reshape(B, T, C)


def shift_seq(x):
    prev = jnp.pad(x, ((0, 0), (1, 0), (0, 0)))[:, :-1]
    nxt = jnp.pad(x, ((0, 0), (0, 1), (0, 0)))[:, 1:]
    return jnp.where(jnp.arange(x.shape[-1]) % 2 == 0, prev, nxt)


def dwconv(x, w, b):
    lo, hi = (CONV_W - 1) // 2, CONV_W // 2
    T = x.shape[1]
    xp = jnp.pad(x, ((0, 0), (lo, hi), (0, 0)))
    y = sum(xp[:, j:j + T] * w[j].astype(x.dtype) for j in range(CONV_W))
    return y + b


RW_L = 64
RW_SUB = 16
RW_PAIRS = RW_HEADS // 2


def _bdot(a, b):
    return jnp.dot(a.astype(jnp.bfloat16), b.astype(jnp.bfloat16), preferred_element_type=jnp.float32)


def _bdot_nt(a, b):
    return lax.dot_general(a.astype(jnp.bfloat16), b.astype(jnp.bfloat16), (((1,), (1,)), ((), ())),
                           preferred_element_type=jnp.float32)


def _rwkv_pair_chunk(lw, r, k, v, kk, a, S):
    L = RW_L
    row = lax.broadcasted_iota(jnp.int32, (L, L), 0)
    col = lax.broadcasted_iota(jnp.int32, (L, L), 1)
    incl = col <= row
    strict = col < row
    diag_blk = (row // RW_SUB) == (col // RW_SUB)
    eye = (row == col).astype(jnp.float32)
    lane = lax.broadcasted_iota(jnp.int32, (1, LANES), 1)
    rowp = lax.broadcasted_iota(jnp.int32, (LANES, LANES), 0)
    colp = lax.broadcasted_iota(jnp.int32, (LANES, LANES), 1)
    same_head = (rowp // RW_HD) == (colp // RW_HD)

    cum = jnp.dot(incl.astype(jnp.float32), lw, precision=lax.Precision.HIGHEST,
                  preferred_element_type=jnp.float32)
    cum_l = cum[L - 1:L, :]
    b = a * kk
    ginv = jnp.exp(-cum)
    gtail = jnp.exp(cum_l - cum)
    kap = kk * jnp.exp(cum - lw)
    bet = b * ginv
    kt = k * ginv
    rt = r * jnp.exp(cum)
    w0 = _bdot_nt(kap, S)
    y0 = _bdot_nt(rt, S)

    us, ys = [], []
    for hh in range(2):
        m = (lane // RW_HD) == hh
        betm = jnp.where(m, bet, 0.0)
        ktm = jnp.where(m, kt, 0.0)
        n = jnp.where(strict, _bdot_nt(kap, betm), 0.0)
        mkk = jnp.where(strict, _bdot_nt(kap, ktm), 0.0)
        mrb = jnp.where(incl, _bdot_nt(rt, betm), 0.0)
        mrk = jnp.where(incl, _bdot_nt(rt, ktm), 0.0)
        d = jnp.where(diag_blk, n, 0.0)
        d2 = _bdot(d, d)
        d4 = _bdot(d2, d2)
        d8 = _bdot(d4, d4)
        tinv = _bdot(_bdot(_bdot(eye - d, eye + d2), eye + d4), eye + d8)
        rhs = w0 + _bdot(mkk, v)
        def rows(done, cur, lo):
            pieces = list(done) + [cur]
            if lo + RW_SUB < L:
                pieces.append(jnp.zeros((L - lo - RW_SUB, LANES), jnp.float32))
            return jnp.concatenate(pieces, axis=0) if len(pieces) > 1 else cur

        zero_blk = jnp.zeros((RW_SUB, LANES), jnp.float32)
        u_blocks = []
        for blk in range(L // RW_SUB):
            lo = blk * RW_SUB
            acc = rhs[lo:lo + RW_SUB, :]
            if blk:
                acc = acc - _bdot(n[lo:lo + RW_SUB, :], rows(u_blocks, zero_blk, lo))
            u_blocks.append(_bdot(tinv[lo:lo + RW_SUB, :], rows([zero_blk] * blk, acc, lo)))
        u = jnp.concatenate(u_blocks, axis=0)
        us.append(u)
        ys.append(y0 - _bdot(mrb, u) + _bdot(mrk, v))
    m0 = (lane // RW_HD) == 0
    u = jnp.where(m0, us[0], us[1])
    y = jnp.where(m0, ys[0], ys[1])
    upd = _bdot(v.T, k * gtail) - _bdot(u.T, b * gtail)
    s_new = S * jnp.exp(cum_l) + jnp.where(same_head, upd, 0.0)
    return y, s_new


def _rwkv_kernel(lw_ref, r_ref, k_ref, v_ref, kk_ref, a_ref, s0_ref, y_ref, sT_ref, s_scr):
    c = pl.program_id(1)

    @pl.when(c == 0)
    def _():
        s_scr[...] = s0_ref[0]

    results = []
    for p in range(RW_PAIRS):
        sl = pl.ds(p * LANES, LANES)
        results.append(_rwkv_pair_chunk(lw_ref[0, :, sl], r_ref[0, :, sl], k_ref[0, :, sl], v_ref[0, :, sl],
                                        kk_ref[0, :, sl], a_ref[0, :, sl], s_scr[p]))
    for p, (y, s_new) in enumerate(results):
        y_ref[0, :, pl.ds(p * LANES, LANES)] = y
        s_scr[p] = s_new

    @pl.when(c == pl.num_programs(1) - 1)
    def _():
        sT_ref[0] = s_scr[...]


def rwkv7_scan(r, lw, k, v, kk, a, S0):
    B, T, C = r.shape
    assert C == RW_PAIRS * LANES and T % RW_L == 0
    seq = pl.BlockSpec((1, RW_L, C), lambda b, c: (b, c, 0))
    st = pl.BlockSpec((1, RW_PAIRS, LANES, LANES), lambda b, c: (b, 0, 0, 0))
    y, sT = pl.pallas_call(
        _rwkv_kernel,
        grid=(B, T // RW_L),
        in_specs=[seq] * 6 + [st],
        out_specs=[seq, st],
        out_shape=[jax.ShapeDtypeStruct((B, T, C), jnp.float32),
                   jax.ShapeDtypeStruct((B, RW_PAIRS, LANES, LANES), jnp.float32)],
        scratch_shapes=[pltpu.VMEM((RW_PAIRS, LANES, LANES), jnp.float32)],
        compiler_params=pltpu.CompilerParams(dimension_semantics=("arbitrary", "arbitrary"),
                                             vmem_limit_bytes=VMEM_LIMIT_BYTES),
        name="rwkv7_scan",
    )(lw, r, k, v, kk, a, S0)
    return y, sT


def rwkv_stream(p, S0, w0, w_up, a0, a_up, g_up, k_k, k_a, r_k, ln_g, ln_b):
    B, T, _ = p.shape
    r, k, v, wd, ad, gd = split_cols(p, [RW_WIDTH, RW_WIDTH, RW_WIDTH, RW_DECAY_RANK, RW_A_RANK])
    heads = lambda t: t.reshape(B, T, RW_HEADS, RW_HD)
    kk = heads(k * k_k).astype(jnp.float32)
    kk = (kk * lax.rsqrt(jnp.maximum(jnp.sum(kk * kk, -1, keepdims=True), 1e-12))).reshape(B, T, RW_WIDTH)
    g = jax.nn.sigmoid(gd) @ g_up
    per_dir, bonus = [], []
    for d in range(2):
        wl = (w0[d] + jnp.tanh(wd) @ w_up[d]).astype(jnp.float32)
        lw = -jnp.exp(-jax.nn.softplus(-wl) - 0.5)
        a = jax.nn.sigmoid(a0[d] + ad @ a_up[d])
        kd = k * (1 + (a - 1) * k_a)
        args = [r, lw, kd, v, kk, a]
        if d == 1:
            args = [flip(t) for t in args]
        per_dir.append(args)
        bonus.append(jnp.sum(heads(r) * heads(kd) * r_k, -1, keepdims=True) * heads(v))
    stacked = [jnp.concatenate([f, b], axis=0) for f, b in zip(*per_dir)]
    y2, sT = rwkv7_scan(*stacked, S0)
    y = (y2[:B] + flip(y2[B:])).astype(jnp.float32).reshape(B, T, RW_HEADS, RW_HD)
    mu = jnp.mean(y, -1, keepdims=True)
    var = jnp.mean(jnp.square(y - mu), -1, keepdims=True)
    yn = ((y - mu) * lax.rsqrt(var + RW_GN_EPS)).reshape(B, T, RW_WIDTH) * ln_g + ln_b
    out = (yn + (bonus[0] + bonus[1]).reshape(B, T, RW_WIDTH)) * g
    return out.astype(p.dtype), sT


def gla_chunked(q, k, v, g, S0):
    B, T, H, K = q.shape
    V = v.shape[-1]
    L = GLA_CHUNK
    n = T // L
    chunks = lambda t: t.astype(jnp.float32).reshape(B, n, L, H, t.shape[-1]).transpose(1, 0, 3, 2, 4)
    mask = jnp.tril(jnp.ones((L, L), bool))

    def step(S, inp):
        qc, kc, vc, gc = inp
        b = jnp.cumsum(gc, axis=2)
        diff = b[:, :, :, None, :] - b[:, :, None, :, :]
        dec = jnp.exp(jnp.where(mask[:, :, None], diff, -jnp.inf))
        A = jnp.einsum("bhik,bhjk,bhijk->bhij", qc, kc, dec)
        o = A @ vc + jnp.einsum("bhik,bhkv->bhiv", qc * jnp.exp(b), S)
        b_last = b[:, :, -1:, :]
        S = jnp.exp(b_last[:, :, 0, :])[..., None] * S + jnp.einsum("bhjk,bhjv->bhkv", kc * jnp.exp(b_last - b), vc)
        return S, o

    S, o = lax.scan(step, S0, (chunks(q), chunks(k), chunks(v), chunks(g)))
    return o.transpose(1, 0, 3, 2, 4).reshape(B, T, H, V).astype(q.dtype), S


def gla_stream(p, S0f, S0b, gate_up, gate_b, norm_g):
    B, T, _ = p.shape
    q, k, v, gd, og = split_cols(p, [GLA_HEADS * GLA_DK, GLA_HEADS * GLA_DK, GLA_HEADS * GLA_DV, GLA_GATE_RANK])
    q = q.reshape(B, T, GLA_HEADS, GLA_DK) * GLA_DK ** -0.5
    k = k.reshape(B, T, GLA_HEADS, GLA_DK)
    v = v.reshape(B, T, GLA_HEADS, GLA_DV)
    outs, states = [], []
    for d, S0 in enumerate((S0f, S0b)):
        glog = jax.nn.log_sigmoid((gd @ gate_up[d] + gate_b[d]).astype(jnp.float32)) / GLA_GATE_NORM
        args = [q, k, v, glog.reshape(B, T, GLA_HEADS, GLA_DK)]
        if d == 1:
            args = [flip(t) for t in args]
        od, Sd = gla_chunked(*args, S0)
        outs.append(flip(od) if d == 1 else od)
        states.append(Sd)
    o = rmsnorm(outs[0] + outs[1], norm_g.reshape(GLA_HEADS, GLA_DV)).reshape(B, T, GLA_HEADS * GLA_DV)
    return (o * jax.nn.silu(og)).astype(p.dtype), states[0], states[1]


def ssd_chunked(x, dt, A, Bm, Cm, S0):
    Bsz, T, H, P = x.shape
    N = Bm.shape[-1]
    L = SSD_CHUNK
    n = T // L
    f = lambda t: t.astype(jnp.float32)
    xc = (f(x) * f(dt)[..., None]).reshape(Bsz, n, L, H, P)
    Bc = f(Bm).reshape(Bsz, n, L, H, N)
    Cc = f(Cm).reshape(Bsz, n, L, H, N)
    acs = jnp.cumsum((f(dt) * A).reshape(Bsz, n, L, H).transpose(0, 3, 1, 2), -1)
    mask = jnp.tril(jnp.ones((L, L), bool))
    Lmat = jnp.exp(jnp.where(mask, acs[..., :, None] - acs[..., None, :], -jnp.inf))
    G = jnp.einsum("bclhn,bcshn->bhcls", Cc, Bc) * Lmat
    y_diag = jnp.einsum("bhcls,bcshp->bclhp", G, xc)
    contrib = jnp.einsum("bclhn,bhcl,bclhp->bchpn", Bc, jnp.exp(acs[..., -1:] - acs), xc)
    chunk_decay = jnp.exp(acs[..., -1])

    def step(S, inp):
        dec, st = inp
        return dec[..., None, None] * S + st, S

    S, S_in = lax.scan(step, S0, (jnp.moveaxis(chunk_decay, 2, 0), jnp.moveaxis(contrib, 1, 0)))
    y_off = jnp.einsum("bclhn,bchpn,bhcl->bclhp", Cc, jnp.moveaxis(S_in, 0, 1), jnp.exp(acs))
    return (y_diag + y_off).reshape(Bsz, T, H, P).astype(x.dtype), S


def ssd_stream(p, S0f, S0b, conv_w, conv_b, dt_bias, A_log, D_skip, norm_g):
    B, T, _ = p.shape
    z, xbc, dtr = split_cols(p, [SSD_INNER, SSD_CONV_CH])
    xbc = jax.nn.silu(dwconv(xbc, conv_w, conv_b))
    xs, Bm, Cm = split_cols(xbc, [SSD_INNER, SSD_GROUPS * SSD_N])
    xs = xs.reshape(B, T, SSD_HEADS, SSD_P)
    rep = SSD_HEADS // SSD_GROUPS
    Bm = jnp.repeat(Bm.reshape(B, T, SSD_GROUPS, SSD_N), rep, axis=2)
    Cm = jnp.repeat(Cm.reshape(B, T, SSD_GROUPS, SSD_N), rep, axis=2)
    ys, states = [], []
    for d, S0 in enumerate((S0f, S0b)):
        dt = jax.nn.softplus((dtr + dt_bias[d]).astype(jnp.float32))
        A = -jnp.exp(A_log[d].astype(jnp.float32))
        args = [xs, dt, Bm, Cm]
        if d == 1:
            args = [flip(t) for t in args]
        yd, Sd = ssd_chunked(args[0], args[1], A, args[2], args[3], S0)
        ys.append(flip(yd) if d == 1 else yd)
        states.append(Sd)
    y = (ys[0] + ys[1] + xs * D_skip[:, None]).reshape(B, T, SSD_INNER) * jax.nn.silu(z)
    return rmsnorm(y, norm_g).astype(p.dtype), states[0], states[1]


def linear_scan(a, b, h0):
    b = b.at[:, 0].add(a[:, 0] * h0)
    comb = lambda l, r: (l[0] * r[0], r[0] * l[1] + r[1])
    _, h = lax.associative_scan(comb, (a, b), axis=1)
    return h, h[:, -1]


def lru_stream(p, h0f, h0b, conv_w, conv_b, wa, ba, wx, bx, lam):
    B, T, _ = p.shape
    xr, gate = split_cols(p, [LRU_WIDTH])
    xf = dwconv(xr, conv_w, conv_b).astype(jnp.float32)
    xh = xf.reshape(B, T, LRU_BLOCKS, LRU_BD)
    hs, finals = [], []
    for d, h0 in enumerate((h0f, h0b)):
        r = jax.nn.sigmoid(jnp.einsum("btgi,gij->btgj", xh, wa[d]).reshape(B, T, LRU_WIDTH) + ba[d])
        i = jax.nn.sigmoid(jnp.einsum("btgi,gij->btgj", xh, wx[d]).reshape(B, T, LRU_WIDTH) + bx[d])
        log_a = -LRU_C * r * jax.nn.softplus(-lam[d])
        a = jnp.exp(log_a)
        b = jnp.sqrt(-jnp.expm1(2 * log_a)) * (i * xf)
        if d == 1:
            a, b = flip(a), flip(b)
        h, hT = linear_scan(a, b, h0)
        hs.append(flip(h) if d == 1 else h)
        finals.append(hT)
    y = (hs[0] + hs[1]).astype(p.dtype) * jax.nn.gelu(gate)
    return y, finals[0], finals[1]


def even_mixer(u, uc, w_in, mu, rw_p, gla_p):
    B = u.shape[0]
    p, pc = mm3(u, w_in), mm3(uc, w_in)
    rw, gl = p[..., :RW_COLS], p[..., RW_COLS:]
    rwc, glc = pc[..., :RW_COLS], pc[..., RW_COLS:]
    rw = rw + mu * (qshift_grid(rw) - rw)
    rwc = rwc + mu * (shift_seq(rwc) - rwc)
    z_rw = jnp.zeros((2 * B, RW_PAIRS, LANES, LANES), jnp.float32)
    yc_rw, s_ctx = rwkv_stream(rwc, z_rw, *rw_p)
    y_rw, _ = rwkv_stream(rw, s_ctx, *rw_p)
    z_gla = jnp.zeros((B, GLA_HEADS, GLA_DK, GLA_DV), jnp.float32)
    yc_gl, Gf, Gb = gla_stream(glc, z_gla, z_gla, *gla_p)
    y_gl, _, _ = gla_stream(gl, Gf, Gb, *gla_p)
    return jnp.concatenate([y_rw, y_gl], -1), jnp.concatenate([yc_rw, yc_gl], -1)


def odd_mixer(u, uc, w_in, ssd_p, lru_p):
    B = u.shape[0]
    p = grid_to_cols(mm3(u, w_in))
    pc = mm3(uc, w_in)
    z_s = jnp.zeros((B, SSD_HEADS, SSD_P, SSD_N), jnp.float32)
    yc_s, Sf, Sb = ssd_stream(pc[..., :SSD_COLS], z_s, z_s, *ssd_p)
    y_s, _, _ = ssd_stream(p[..., :SSD_COLS], Sf, Sb, *ssd_p)
    z_l = jnp.zeros((B, LRU_WIDTH), jnp.float32)
    yc_l, hf, hb = lru_stream(pc[..., SSD_COLS:], z_l, z_l, *lru_p)
    y_l, _, _ = lru_stream(p[..., SSD_COLS:], hf, hb, *lru_p)
    return cols_to_grid(jnp.concatenate([y_s, y_l], -1)), jnp.concatenate([yc_s, yc_l], -1)


def peer(x, wq, keys, u_tab, v_tab):
    B, T, D = x.shape
    M = B * T
    xf = x.reshape(M, D)
    idx, gate = peer_route(xf, wq, keys)
    return peer_experts(xf, idx, gate, u_tab, v_tab).reshape(B, T, D)


def kernel(x, c, ctx, c_ctx, ada_w, ada_b, norm1_g, norm2_g, ev_w_in, rw_mu, rw_w0, rw_w_up, rw_a0, rw_a_up,
           rw_g_up, rw_k_k, rw_k_a, rw_r_k, rw_ln_g, rw_ln_b, gla_gate_up, gla_gate_b, gla_norm_g, od_w_in,
           ssd_conv_w, ssd_conv_b, ssd_dt_bias, ssd_A_log, ssd_D, ssd_norm_g, lru_conv_w, lru_conv_b, lru_wa,
           lru_ba, lru_wx, lru_bx, lru_lam, w_out, peer_wq, peer_keys, peer_u, peer_v, final_g):
    h, hc = x, ctx
    sc, scc = jax.nn.silu(c), jax.nn.silu(c_ctx)
    for l in range(DEPTH):
        last = l == DEPTH - 1
        i = l // 2
        mod = jnp.split((sc @ ada_w[l] + ada_b[l])[:, None, :], 6, axis=-1)
        modc = jnp.split(scc @ ada_w[l] + ada_b[l], 6, axis=-1)
        u = modulate(h, norm1_g[l], mod[0], mod[1])
        uc = modulate(hc, norm1_g[l], modc[0], modc[1])
        if l % 2 == 0:
            rw_p = (rw_w0[i], rw_w_up[i], rw_a0[i], rw_a_up[i], rw_g_up[i], rw_k_k[i], rw_k_a[i], rw_r_k[i],
                    rw_ln_g[i], rw_ln_b[i])
            gla_p = (gla_gate_up[i], gla_gate_b[i], gla_norm_g[i])
            y, yc = even_mixer(u, uc, ev_w_in[i], rw_mu[i], rw_p, gla_p)
        else:
            ssd_p = (ssd_conv_w[i], ssd_conv_b[i], ssd_dt_bias[i], ssd_A_log[i], ssd_D[i], ssd_norm_g[i])
            lru_p = (lru_conv_w[i], lru_conv_b[i], lru_wa[i], lru_ba[i], lru_wx[i], lru_bx[i], lru_lam[i])
            y, yc = odd_mixer(u, uc, od_w_in[i], ssd_p, lru_p)
        h = h + mod[2] * mm3(y, w_out[l])
        u = modulate(h, norm2_g[l], mod[3], mod[4])
        h = h + mod[5] * peer(u, peer_wq[l], peer_keys[l], peer_u[l], peer_v[l])
        if not last:
            hc = hc + modc[2] * mm3(yc, w_out[l])
            uc = modulate(hc, norm2_g[l], modc[3], modc[4])
            hc = hc + modc[5] * peer(uc, peer_wq[l], peer_keys[l], peer_u[l], peer_v[l])
    return rmsnorm(h, final_g)
```

```python
import functools

import jax
import jax.numpy as jnp
from jax import lax
import numpy as np
from jax.experimental import pallas as pl
from jax.experimental.pallas import tpu as pltpu

D_MODEL = 1024
DEPTH = 4
GRID_W = 64
MIX_WIDTH = D_MODEL

RW_WIDTH = MIX_WIDTH // 2
RW_HD = 64
RW_HEADS = RW_WIDTH // RW_HD
RW_DECAY_RANK = 32
RW_A_RANK = 32
RW_GATE_RANK = 64
RW_GN_EPS = 64e-5
RW_COLS = 3 * RW_WIDTH + RW_DECAY_RANK + RW_A_RANK + RW_GATE_RANK
GLA_HEADS = 4
GLA_DV = (MIX_WIDTH - RW_WIDTH) // GLA_HEADS
GLA_DK = GLA_DV // 2
GLA_GATE_RANK = 16
GLA_GATE_NORM = 16.0
GLA_CHUNK = 64
SSD_INNER = MIX_WIDTH // 2
SSD_P = 64
SSD_HEADS = SSD_INNER // SSD_P
SSD_N = 128
SSD_GROUPS = 2
SSD_CHUNK = 128
SSD_CONV_CH = SSD_INNER + 2 * SSD_GROUPS * SSD_N
SSD_COLS = SSD_INNER + SSD_CONV_CH + SSD_HEADS
LRU_WIDTH = MIX_WIDTH - SSD_INNER
LRU_BLOCKS = 8
LRU_BD = LRU_WIDTH // LRU_BLOCKS
LRU_C = 8.0
CONV_W = 4
PEER_HEADS = 8
N_KEYS = 128
PEER_TOPK = 16
PEER_DQ = 256
PEER_BLOCK = 128

VMEM_LIMIT_BYTES = 56 * 1024 * 1024


def _mm_kernel(a_ref, b_ref, o_ref):
    o_ref[...] = jnp.dot(a_ref[...].astype(jnp.bfloat16), b_ref[...].astype(jnp.bfloat16),
                         preferred_element_type=jnp.float32)


def pmm(a, b, tm=256):
    M, K = a.shape
    N = b.shape[1]
    assert M % tm == 0
    return pl.pallas_call(
        _mm_kernel,
        grid=(M // tm,),
        in_specs=[pl.BlockSpec((tm, K), lambda i: (i, 0)), pl.BlockSpec((K, N), lambda i: (0, 0))],
        out_specs=pl.BlockSpec((tm, N), lambda i: (i, 0)),
        out_shape=jax.ShapeDtypeStruct((M, N), jnp.float32),
        compiler_params=pltpu.CompilerParams(dimension_semantics=("arbitrary",),
                                             vmem_limit_bytes=VMEM_LIMIT_BYTES),
        name="pmm",
    )(a, b)


SUBLANES = 8
LANES = 128
PEER_E = PEER_HEADS * PEER_TOPK
PEER_TB = 8


def _fold8(p):
    sub = lax.broadcasted_iota(jnp.int32, (SUBLANES, LANES), 0)

    def comb(a, b, k):
        fa = a + pltpu.roll(a, SUBLANES - k, 0)
        fb = b + pltpu.roll(b, k, 0)
        return jnp.where((sub & k) == 0, fa, fb)

    return comb(comb(comb(p[0], p[4], 4), comb(p[2], p[6], 4), 2),
                comb(comb(p[1], p[5], 4), comb(p[3], p[7], 4), 2), 1)


def _peer_kernel(idx_cur_ref, idx_nxt_ref, gate_ref, x_ref, tab_ref, o_ref, buf, sem, cs_ref):
    i = pl.program_id(0)
    n = pl.num_programs(0)
    slot = i % 2
    nslot = 1 - slot

    def issue(idx_ref, t, s):
        for e in range(PEER_E):
            j = t * PEER_E + e
            pltpu.make_async_copy(tab_ref.at[idx_ref[0, 0, j]], buf.at[s, j], sem.at[s]).start(priority=e % 2)

    def wait_slot(s):
        pltpu.make_async_copy(buf.at[s], buf.at[s], sem.at[s]).wait()

    @pl.when(i == 0)
    def _():
        lax.fori_loop(0, PEER_TB, lambda t, c: (issue(idx_cur_ref, t, 0), c)[1], 0)

    wait_slot(slot)

    def token(t, carry):
        issue(idx_nxt_ref, t, nslot)
        xt = x_ref[t]
        base = t * PEER_E
        for g in range(PEER_E // SUBLANES):
            s_g = _fold8([buf[slot, base + g * SUBLANES + j, 0] * xt for j in range(SUBLANES)])
            a = jnp.sum(s_g, axis=1, keepdims=True)
            cs_ref[pl.ds(g * SUBLANES, SUBLANES), :] = jnp.broadcast_to(jax.nn.gelu(a), (SUBLANES, LANES))
        acc = jnp.zeros((SUBLANES, LANES), jnp.float32)
        for e in range(PEER_E):
            c = cs_ref[pl.ds(e, 1), :] * gate_ref[0, 0, base + e]
            acc = acc + c * buf[slot, base + e, 1]
        o_ref[t] = acc
        return carry

    lax.fori_loop(0, PEER_TB, token, 0)

    @pl.when(i == n - 1)
    def _():
        wait_slot(nslot)


def peer_experts(xf, idx, gate, u_tab, v_tab):
    M, D = xf.shape
    NE = u_tab.shape[0]
    assert D == SUBLANES * LANES and M % PEER_TB == 0 and idx.shape == (M, PEER_E)
    nblk = M // PEER_TB
    tab = jnp.stack([u_tab.reshape(NE, SUBLANES, LANES), v_tab.reshape(NE, SUBLANES, LANES)], axis=1)
    idx3 = idx.astype(jnp.int32).reshape(nblk, 1, PEER_TB * PEER_E)
    gate3 = gate.astype(jnp.float32).reshape(nblk, 1, PEER_TB * PEER_E)
    x3 = xf.reshape(M, SUBLANES, LANES)
    smem_blk = (1, 1, PEER_TB * PEER_E)
    out = pl.pallas_call(
        _peer_kernel,
        grid=(nblk,),
        in_specs=[
            pl.BlockSpec(smem_blk, lambda i: (i, 0, 0), memory_space=pltpu.SMEM),
            pl.BlockSpec(smem_blk, lambda i: (jnp.minimum(i + 1, nblk - 1), 0, 0), memory_space=pltpu.SMEM),
            pl.BlockSpec(smem_blk, lambda i: (i, 0, 0), memory_space=pltpu.SMEM),
            pl.BlockSpec((PEER_TB, SUBLANES, LANES), lambda i: (i, 0, 0)),
            pl.BlockSpec(memory_space=pl.ANY),
        ],
        out_specs=pl.BlockSpec((PEER_TB, SUBLANES, LANES), lambda i: (i, 0, 0)),
        out_shape=jax.ShapeDtypeStruct((M, SUBLANES, LANES), jnp.float32),
        scratch_shapes=[
            pltpu.VMEM((2, PEER_TB * PEER_E, 2, SUBLANES, LANES), jnp.float32),
            pltpu.SemaphoreType.DMA((2,)),
            pltpu.VMEM((PEER_E, LANES), jnp.float32),
        ],
        compiler_params=pltpu.CompilerParams(dimension_semantics=("arbitrary",),
                                             vmem_limit_bytes=VMEM_LIMIT_BYTES),
        name="peer_experts",
    )(idx3, idx3, gate3, x3, tab)
    return out.reshape(M, D)


ROUTE_TM = 128
PEER_HALF = PEER_DQ // 2


def _topk_rows(s, payload, k):
    n = s.shape[0]
    rid = lax.broadcasted_iota(jnp.int32, s.shape, 0)
    vals, sel = [], []
    for _ in range(k):
        m = jnp.max(s, axis=0, keepdims=True)
        pos = jnp.min(jnp.where(s == m, rid, n), axis=0, keepdims=True)
        hit = rid == pos
        vals.append(m)
        sel.append(pos if payload is None else jnp.max(jnp.where(hit, payload, -1), axis=0, keepdims=True))
        s = jnp.where(hit, -jnp.inf, s)
    return jnp.concatenate(vals, axis=0), jnp.concatenate(sel, axis=0)


def _route_kernel(x_ref, wq_ref, keys_ref, idx_ref, gate_ref):
    q = jnp.dot(x_ref[...].astype(jnp.bfloat16), wq_ref[...], preferred_element_type=jnp.float32)
    for h in range(PEER_HEADS):
        top_s, top_i = [], []
        for half in range(2):
            c0 = (h * 2 + half) * PEER_HALF
            qh = q[:, c0:c0 + PEER_HALF].astype(jnp.bfloat16)
            s = lax.dot_general(keys_ref[h, half], qh, (((1,), (1,)), ((), ())),
                                preferred_element_type=jnp.float32)
            ts, ti = _topk_rows(s, None, PEER_TOPK)
            top_s.append(ts)
            top_i.append(ti)
        cand_s = jnp.concatenate([top_s[0][i:i + 1, :] + top_s[1] for i in range(PEER_TOPK)], axis=0)
        cand_i = jnp.concatenate([top_i[0][i:i + 1, :] * N_KEYS + top_i[1] for i in range(PEER_TOPK)], axis=0)
        best_s, best_i = _topk_rows(cand_s, cand_i, PEER_TOPK)
        e = jnp.exp(best_s - best_s[0:1, :])
        r0 = h * PEER_TOPK
        gate_ref[0, r0:r0 + PEER_TOPK, :] = e / jnp.sum(e, axis=0, keepdims=True)
        idx_ref[0, r0:r0 + PEER_TOPK, :] = best_i


def peer_route(xf, wq, keys):
    M, D = xf.shape
    assert M % ROUTE_TM == 0
    nblk = M // ROUTE_TM
    out_blk = pl.BlockSpec((1, PEER_E, ROUTE_TM), lambda i: (i, 0, 0))
    idx, gate = pl.pallas_call(
        _route_kernel,
        grid=(nblk,),
        in_specs=[pl.BlockSpec((ROUTE_TM, D), lambda i: (i, 0)),
                  pl.BlockSpec(wq.shape, lambda i: (0, 0)),
                  pl.BlockSpec(keys.shape, lambda i: (0, 0, 0, 0))],
        out_specs=[out_blk, out_blk],
        out_shape=[jax.ShapeDtypeStruct((nblk, PEER_E, ROUTE_TM), jnp.int32),
                   jax.ShapeDtypeStruct((nblk, PEER_E, ROUTE_TM), jnp.float32)],
        compiler_params=pltpu.CompilerParams(dimension_semantics=("arbitrary",),
                                             vmem_limit_bytes=VMEM_LIMIT_BYTES),
        name="peer_route",
    )(xf, wq.astype(jnp.bfloat16), keys.astype(jnp.bfloat16))
    to_tokens = lambda t: t.transpose(0, 2, 1).reshape(M, PEER_E)
    return to_tokens(idx), to_tokens(gate)


def mm3(x, w):
    B, T, K = x.shape
    return pmm(x.reshape(B * T, K), w).reshape(B, T, w.shape[1])


def rmsnorm(x, g, eps=1e-6):
    xf = x.astype(jnp.float32)
    y = xf * lax.rsqrt(jnp.mean(xf * xf, -1, keepdims=True) + eps)
    return (y * g.astype(jnp.float32)).astype(x.dtype)


def modulate(x, g, shift, scale):
    return rmsnorm(x, g) * (1 + scale) + shift


def split_cols(p, sizes):
    return jnp.split(p, [int(i) for i in np.cumsum(sizes)], axis=-1)


def flip(t):
    return jnp.flip(t, axis=1)


def grid_to_cols(x):
    B, T, C = x.shape
    rows = T // GRID_W
    return x.reshape(B, rows, GRID_W, C).transpose(0, 2, 1, 3).reshape(B, T, C)


def cols_to_grid(x):
    B, T, C = x.shape
    rows = T // GRID_W
    return x.reshape(B, GRID_W, rows, C).transpose(0, 2, 1, 3).reshape(B, T, C)


def qshift_grid(x):
    B, T, C = x.shape
    rows = T // GRID_W
    g = jnp.pad(x.reshape(B, rows, GRID_W, C), ((0, 0), (1, 1), (1, 1), (0, 0)))
    sel = jnp.arange(C) % 4
    out = jnp.where(sel == 0, g[:, 1:-1, :-2], jnp.where(sel == 1, g[:, 1:-1, 2:],
                    jnp.where(sel == 2, g[:, :-2, 1:-1], g[:, 2:, 1:-1])))
    return out.reshape(B, T, C)


def shift_seq(x):
    prev = jnp.pad(x, ((0, 0), (1, 0), (0, 0)))[:, :-1]
    nxt = jnp.pad(x, ((0, 0), (0, 1), (0, 0)))[:, 1:]
    return jnp.where(jnp.arange(x.shape[-1]) % 2 == 0, prev, nxt)


def dwconv(x, w, b):
    lo, hi = (CONV_W - 1) // 2, CONV_W // 2
    T = x.shape[1]
    xp = jnp.pad(x, ((0, 0), (lo, hi), (0, 0)))
    y = sum(xp[:, j:j + T] * w[j].astype(x.dtype) for j in range(CONV_W))
    return y + b


RW_L = 64
RW_SUB = 16
RW_PAIRS = RW_HEADS // 2


def _bdot(a, b):
    return jnp.dot(a.astype(jnp.bfloat16), b.astype(jnp.bfloat16), preferred_element_type=jnp.float32)


def _bdot_nt(a, b):
    return lax.dot_general(a.astype(jnp.bfloat16), b.astype(jnp.bfloat16), (((1,), (1,)), ((), ())),
                           preferred_element_type=jnp.float32)


def _rwkv_chunk(lws, rs, ks, vs, kks, as_, states):
    L, L2 = RW_L, 2 * RW_L
    P = range(len(lws))
    row = lax.broadcasted_iota(jnp.int32, (L, L), 0)
    col = lax.broadcasted_iota(jnp.int32, (L, L), 1)
    tri = (col <= row).astype(jnp.float32)
    row2 = lax.broadcasted_iota(jnp.int32, (L2, L2), 0)
    col2 = lax.broadcasted_iota(jnp.int32, (L2, L2), 1)
    same = (row2 // L) == (col2 // L)
    strict = same & (col2 < row2)
    incl = same & (col2 <= row2)
    diag_blk = (row2 // RW_SUB) == (col2 // RW_SUB)
    eye = (row2 == col2).astype(jnp.float32)
    head_a = (lax.broadcasted_iota(jnp.int32, (1, LANES), 1) // RW_HD) == 0

    def stack2(x):
        return jnp.concatenate([jnp.where(head_a, x, 0.0), jnp.where(head_a, 0.0, x)], axis=0)

    cum = [jnp.dot(tri, lws[p], precision=lax.Precision.HIGHEST, preferred_element_type=jnp.float32) for p in P]
    cum_l = [cum[p][L - 1:L, :] for p in P]
    b = [as_[p] * kks[p] for p in P]
    ginv = [jnp.exp(-cum[p]) for p in P]
    gtail = [jnp.exp(cum_l[p] - cum[p]) for p in P]
    left = [jnp.concatenate([stack2(kks[p] * jnp.exp(cum[p] - lws[p])), stack2(rs[p] * jnp.exp(cum[p]))], axis=0)
            for p in P]
    right = [jnp.concatenate([stack2(b[p] * ginv[p]), stack2(ks[p] * ginv[p])], axis=0) for p in P]
    v2 = [stack2(vs[p]) for p in P]
    g = [_bdot_nt(left[p], right[p]) for p in P]
    wy = [_bdot_nt(left[p], states[p]) for p in P]
    n = [jnp.where(strict, g[p][:L2, :L2], 0.0) for p in P]
    mkk = [jnp.where(strict, g[p][:L2, L2:], 0.0) for p in P]
    mrb = [jnp.where(incl, g[p][L2:, :L2], 0.0) for p in P]
    mrk = [jnp.where(incl, g[p][L2:, L2:], 0.0) for p in P]
    d = [jnp.where(diag_blk, n[p], 0.0) for p in P]
    d2 = [_bdot(d[p], d[p]) for p in P]
    d4 = [_bdot(d2[p], d2[p]) for p in P]
    d8 = [_bdot(d4[p], d4[p]) for p in P]
    tinv = [_bdot(eye - d[p], eye + d2[p]) for p in P]
    tinv = [_bdot(tinv[p], eye + d4[p]) for p in P]
    tinv = [_bdot(tinv[p], eye + d8[p]) for p in P]
    rhs = [wy[p][:L2] + _bdot(mkk[p], v2[p]) for p in P]

    nblk = L // RW_SUB
    zero_blk = jnp.zeros((RW_SUB, LANES), jnp.float32)

    def pick(x, blk):
        lo = blk * RW_SUB
        return jnp.concatenate([x[lo:lo + RW_SUB], x[L + lo:L + lo + RW_SUB]], axis=0)

    def place(blocks):
        half = lambda h: [zero_blk if blocks[j] is None else blocks[j][h * RW_SUB:(h + 1) * RW_SUB]
                          for j in range(nblk)]
        return jnp.concatenate(half(0) + half(1), axis=0)

    u_blocks = [[None] * nblk for _ in P]
    for blk in range(nblk):
        for p in P:
            acc = pick(rhs[p], blk)
            if blk:
                acc = acc - _bdot(pick(n[p], blk), place(u_blocks[p]))
            cur = [None] * nblk
            cur[blk] = acc
            u_blocks[p][blk] = _bdot(pick(tinv[p], blk), place(cur))
    u = [place(u_blocks[p]) for p in P]
    y2 = [wy[p][L2:] - _bdot(mrb[p], u[p]) + _bdot(mrk[p], v2[p]) for p in P]
    ys = [y2[p][:L] + y2[p][L:] for p in P]
    upd = [_bdot(v2[p].T, stack2(ks[p] * gtail[p])) - _bdot(u[p].T, stack2(b[p] * gtail[p])) for p in P]
    s_news = [states[p] * jnp.exp(cum_l[p]) + upd[p] for p in P]
    return ys, s_news


def _rwkv_kernel(lw_ref, r_ref, k_ref, v_ref, kk_ref, a_ref, s0_ref, y_ref, sT_ref, s_scr):
    c = pl.program_id(1)

    @pl.when(c == 0)
    def _():
        s_scr[...] = s0_ref[0]

    slabs = [pl.ds(p * LANES, LANES) for p in range(RW_PAIRS)]
    ys, s_news = _rwkv_chunk(*[[ref[0, :, sl] for sl in slabs]
                               for ref in (lw_ref, r_ref, k_ref, v_ref, kk_ref, a_ref)],
                             [s_scr[p] for p in range(RW_PAIRS)])
    for p in range(RW_PAIRS):
        y_ref[0, :, slabs[p]] = ys[p]
        s_scr[p] = s_news[p]

    @pl.when(c == pl.num_programs(1) - 1)
    def _():
        sT_ref[0] = s_scr[...]


def rwkv7_scan(r, lw, k, v, kk, a, S0):
    B, T, C = r.shape
    assert C == RW_PAIRS * LANES and T % RW_L == 0
    seq = pl.BlockSpec((1, RW_L, C), lambda b, c: (b, c, 0))
    st = pl.BlockSpec((1, RW_PAIRS, LANES, LANES), lambda b, c: (b, 0, 0, 0))
    y, sT = pl.pallas_call(
        _rwkv_kernel,
        grid=(B, T // RW_L),
        in_specs=[seq] * 6 + [st],
        out_specs=[seq, st],
        out_shape=[jax.ShapeDtypeStruct((B, T, C), jnp.float32),
                   jax.ShapeDtypeStruct((B, RW_PAIRS, LANES, LANES), jnp.float32)],
        scratch_shapes=[pltpu.VMEM((RW_PAIRS, LANES, LANES), jnp.float32)],
        compiler_params=pltpu.CompilerParams(dimension_semantics=("arbitrary", "arbitrary"),
                                             vmem_limit_bytes=VMEM_LIMIT_BYTES),
        name="rwkv7_scan",
    )(lw, r, k, v, kk, a, S0)
    return y, sT


def rwkv_stream(p, S0, w0, w_up, a0, a_up, g_up, k_k, k_a, r_k, ln_g, ln_b):
    B, T, _ = p.shape
    r, k, v, wd, ad, gd = split_cols(p, [RW_WIDTH, RW_WIDTH, RW_WIDTH, RW_DECAY_RANK, RW_A_RANK])
    heads = lambda t: t.reshape(B, T, RW_HEADS, RW_HD)
    kk = heads(k * k_k).astype(jnp.float32)
    kk = (kk * lax.rsqrt(jnp.maximum(jnp.sum(kk * kk, -1, keepdims=True), 1e-12))).reshape(B, T, RW_WIDTH)
    g = jax.nn.sigmoid(gd) @ g_up
    per_dir, bonus = [], []
    for d in range(2):
        wl = (w0[d] + jnp.tanh(wd) @ w_up[d]).astype(jnp.float32)
        lw = -jnp.exp(-jax.nn.softplus(-wl) - 0.5)
        a = jax.nn.sigmoid(a0[d] + ad @ a_up[d])
        kd = k * (1 + (a - 1) * k_a)
        args = [r, lw, kd, v, kk, a]
        if d == 1:
            args = [flip(t) for t in args]
        per_dir.append(args)
        bonus.append(jnp.sum(heads(r) * heads(kd) * r_k, -1, keepdims=True) * heads(v))
    stacked = [jnp.concatenate([f, b], axis=0) for f, b in zip(*per_dir)]
    y2, sT = rwkv7_scan(*stacked, S0)
    y = (y2[:B] + flip(y2[B:])).astype(jnp.float32).reshape(B, T, RW_HEADS, RW_HD)
    mu = jnp.mean(y, -1, keepdims=True)
    var = jnp.mean(jnp.square(y - mu), -1, keepdims=True)
    yn = ((y - mu) * lax.rsqrt(var + RW_GN_EPS)).reshape(B, T, RW_WIDTH) * ln_g + ln_b
    out = (yn + (bonus[0] + bonus[1]).reshape(B, T, RW_WIDTH)) * g
    return out.astype(p.dtype), sT


GLA_SUB = 16
GLA_PAIRS = GLA_HEADS // 2
assert GLA_DV == LANES and 2 * GLA_DK == LANES


def _gla_kernel(q_ref, k_ref, g_ref, v_ref, s0_ref, o_ref, sT_ref, s_scr):
    c = pl.program_id(1)
    L = GLA_CHUNK

    @pl.when(c == 0)
    def _():
        s_scr[...] = s0_ref[0]

    row = lax.broadcasted_iota(jnp.int32, (L, L), 0)
    col = lax.broadcasted_iota(jnp.int32, (L, L), 1)
    incl = col <= row
    lane = lax.broadcasted_iota(jnp.int32, (1, LANES), 1)
    trow = lax.broadcasted_iota(jnp.int32, (L, LANES), 0)
    outs = []
    for p in range(GLA_PAIRS):
        sl = pl.ds(p * LANES, LANES)
        q, k, g = q_ref[0, :, sl], k_ref[0, :, sl], g_ref[0, :, sl]
        b = jnp.dot(incl.astype(jnp.float32), g, precision=lax.Precision.HIGHEST,
                    preferred_element_type=jnp.float32)
        b_l = b[L - 1:L, :]
        qe = q * jnp.exp(b)
        kd = k * jnp.exp(b_l - b)
        qfs, kfs = [], []
        for blk in range(L // GLA_SUB):
            lo, hi = blk * GLA_SUB, (blk + 1) * GLA_SUB
            b_ref = b[lo:lo + 1, :]
            qfs.append(q[lo:hi, :] * jnp.exp(b[lo:hi, :] - b_ref))
            used = trow < hi
            kfs.append(jnp.where(used, k * jnp.exp(jnp.where(used, b_ref - b, 0.0)), 0.0))
        for hh in range(2):
            h = 2 * p + hh
            m = (lane // GLA_DK) == hh
            v = v_ref[0, :, pl.ds(h * GLA_DV, GLA_DV)]
            s_t = s_scr[h]
            a = jnp.concatenate([_bdot_nt(qf, jnp.where(m, kf, 0.0)) for qf, kf in zip(qfs, kfs)], axis=0)
            o = _bdot(jnp.where(incl, a, 0.0), v) + _bdot_nt(qe, s_t)
            s_new = s_t * jnp.exp(b_l) + _bdot(v.T, jnp.where(m, kd, 0.0))
            outs.append((h, o, s_new))
    for h, o, s_new in outs:
        o_ref[0, :, pl.ds(h * GLA_DV, GLA_DV)] = o
        s_scr[h] = s_new

    @pl.when(c == pl.num_programs(1) - 1)
    def _():
        sT_ref[0] = s_scr[...]


def gla_chunked(q, k, g, v, S0):
    B, T, _ = q.shape
    assert T % GLA_CHUNK == 0
    qk = pl.BlockSpec((1, GLA_CHUNK, GLA_HEADS * GLA_DK), lambda b, c: (b, c, 0))
    vv = pl.BlockSpec((1, GLA_CHUNK, GLA_HEADS * GLA_DV), lambda b, c: (b, c, 0))
    st = pl.BlockSpec((1, GLA_HEADS, LANES, LANES), lambda b, c: (b, 0, 0, 0))
    return pl.pallas_call(
        _gla_kernel,
        grid=(B, T // GLA_CHUNK),
        in_specs=[qk, qk, qk, vv, st],
        out_specs=[vv, st],
        out_shape=[jax.ShapeDtypeStruct((B, T, GLA_HEADS * GLA_DV), jnp.float32),
                   jax.ShapeDtypeStruct((B, GLA_HEADS, LANES, LANES), jnp.float32)],
        scratch_shapes=[pltpu.VMEM((GLA_HEADS, LANES, LANES), jnp.float32)],
        compiler_params=pltpu.CompilerParams(dimension_semantics=("arbitrary", "arbitrary"),
                                             vmem_limit_bytes=VMEM_LIMIT_BYTES),
        name="gla_chunked",
    )(q, k, g, v, S0)


def gla_stream(p, S0, gate_up, gate_b, norm_g):
    B, T, _ = p.shape
    q, k, v, gd, og = split_cols(p, [GLA_HEADS * GLA_DK, GLA_HEADS * GLA_DK, GLA_HEADS * GLA_DV, GLA_GATE_RANK])
    q = q * GLA_DK ** -0.5
    per_dir = []
    for d in range(2):
        glog = jax.nn.log_sigmoid((gd @ gate_up[d] + gate_b[d]).astype(jnp.float32)) / GLA_GATE_NORM
        args = [q, k, glog, v]
        if d == 1:
            args = [flip(t) for t in args]
        per_dir.append(args)
    stacked = [jnp.concatenate([f, b], axis=0) for f, b in zip(*per_dir)]
    o2, sT = gla_chunked(*stacked, S0)
    o = (o2[:B] + flip(o2[B:])).reshape(B, T, GLA_HEADS, GLA_DV)
    o = rmsnorm(o, norm_g.reshape(GLA_HEADS, GLA_DV)).reshape(B, T, GLA_HEADS * GLA_DV)
    return (o * jax.nn.silu(og)).astype(p.dtype), sT


def ssd_chunked(x, dt, A, Bm, Cm, S0):
    Bsz, T, H, P = x.shape
    N = Bm.shape[-1]
    L = SSD_CHUNK
    n = T // L
    f = lambda t: t.astype(jnp.float32)
    xc = (f(x) * f(dt)[..., None]).reshape(Bsz, n, L, H, P)
    Bc = f(Bm).reshape(Bsz, n, L, H, N)
    Cc = f(Cm).reshape(Bsz, n, L, H, N)
    acs = jnp.cumsum((f(dt) * A).reshape(Bsz, n, L, H).transpose(0, 3, 1, 2), -1)
    mask = jnp.tril(jnp.ones((L, L), bool))
    Lmat = jnp.exp(jnp.where(mask, acs[..., :, None] - acs[..., None, :], -jnp.inf))
    G = jnp.einsum("bclhn,bcshn->bhcls", Cc, Bc) * Lmat
    y_diag = jnp.einsum("bhcls,bcshp->bclhp", G, xc)
    contrib = jnp.einsum("bclhn,bhcl,bclhp->bchpn", Bc, jnp.exp(acs[..., -1:] - acs), xc)
    chunk_decay = jnp.exp(acs[..., -1])

    def step(S, inp):
        dec, st = inp
        return dec[..., None, None] * S + st, S

    S, S_in = lax.scan(step, S0, (jnp.moveaxis(chunk_decay, 2, 0), jnp.moveaxis(contrib, 1, 0)))
    y_off = jnp.einsum("bclhn,bchpn,bhcl->bclhp", Cc, jnp.moveaxis(S_in, 0, 1), jnp.exp(acs))
    return (y_diag + y_off).reshape(Bsz, T, H, P).astype(x.dtype), S


def ssd_stream(p, S0f, S0b, conv_w, conv_b, dt_bias, A_log, D_skip, norm_g):
    B, T, _ = p.shape
    z, xbc, dtr = split_cols(p, [SSD_INNER, SSD_CONV_CH])
    xbc = jax.nn.silu(dwconv(xbc, conv_w, conv_b))
    xs, Bm, Cm = split_cols(xbc, [SSD_INNER, SSD_GROUPS * SSD_N])
    xs = xs.reshape(B, T, SSD_HEADS, SSD_P)
    rep = SSD_HEADS // SSD_GROUPS
    Bm = jnp.repeat(Bm.reshape(B, T, SSD_GROUPS, SSD_N), rep, axis=2)
    Cm = jnp.repeat(Cm.reshape(B, T, SSD_GROUPS, SSD_N), rep, axis=2)
    ys, states = [], []
    for d, S0 in enumerate((S0f, S0b)):
        dt = jax.nn.softplus((dtr + dt_bias[d]).astype(jnp.float32))
        A = -jnp.exp(A_log[d].astype(jnp.float32))
        args = [xs, dt, Bm, Cm]
        if d == 1:
            args = [flip(t) for t in args]
        yd, Sd = ssd_chunked(args[0], args[1], A, args[2], args[3], S0)
        ys.append(flip(yd) if d == 1 else yd)
        states.append(Sd)
    y = (ys[0] + ys[1] + xs * D_skip[:, None]).reshape(B, T, SSD_INNER) * jax.nn.silu(z)
    return rmsnorm(y, norm_g).astype(p.dtype), states[0], states[1]


def linear_scan(a, b, h0):
    b = b.at[:, 0].add(a[:, 0] * h0)
    comb = lambda l, r: (l[0] * r[0], r[0] * l[1] + r[1])
    _, h = lax.associative_scan(comb, (a, b), axis=1)
    return h, h[:, -1]


def lru_stream(p, h0f, h0b, conv_w, conv_b, wa, ba, wx, bx, lam):
    B, T, _ = p.shape
    xr, gate = split_cols(p, [LRU_WIDTH])
    xf = dwconv(xr, conv_w, conv_b).astype(jnp.float32)
    xh = xf.reshape(B, T, LRU_BLOCKS, LRU_BD)
    hs, finals = [], []
    for d, h0 in enumerate((h0f, h0b)):
        r = jax.nn.sigmoid(jnp.einsum("btgi,gij->btgj", xh, wa[d]).reshape(B, T, LRU_WIDTH) + ba[d])
        i = jax.nn.sigmoid(jnp.einsum("btgi,gij->btgj", xh, wx[d]).reshape(B, T, LRU_WIDTH) + bx[d])
        log_a = -LRU_C * r * jax.nn.softplus(-lam[d])
        a = jnp.exp(log_a)
        b = jnp.sqrt(-jnp.expm1(2 * log_a)) * (i * xf)
        if d == 1:
            a, b = flip(a), flip(b)
        h, hT = linear_scan(a, b, h0)
        hs.append(flip(h) if d == 1 else h)
        finals.append(hT)
    y = (hs[0] + hs[1]).astype(p.dtype) * jax.nn.gelu(gate)
    return y, finals[0], finals[1]


def even_mixer(u, uc, w_in, mu, rw_p, gla_p):
    B = u.shape[0]
    p, pc = mm3(u, w_in), mm3(uc, w_in)
    rw, gl = p[..., :RW_COLS], p[..., RW_COLS:]
    rwc, glc = pc[..., :RW_COLS], pc[..., RW_COLS:]
    rw = rw + mu * (qshift_grid(rw) - rw)
    rwc = rwc + mu * (shift_seq(rwc) - rwc)
    z_rw = jnp.zeros((2 * B, RW_PAIRS, LANES, LANES), jnp.float32)
    yc_rw, s_ctx = rwkv_stream(rwc, z_rw, *rw_p)
    y_rw, _ = rwkv_stream(rw, s_ctx, *rw_p)
    z_gla = jnp.zeros((2 * B, GLA_HEADS, LANES, LANES), jnp.float32)
    yc_gl, g_ctx = gla_stream(glc, z_gla, *gla_p)
    y_gl, _ = gla_stream(gl, g_ctx, *gla_p)
    return jnp.concatenate([y_rw, y_gl], -1), jnp.concatenate([yc_rw, yc_gl], -1)


def odd_mixer(u, uc, w_in, ssd_p, lru_p):
    B = u.shape[0]
    p = grid_to_cols(mm3(u, w_in))
    pc = mm3(uc, w_in)
    z_s = jnp.zeros((B, SSD_HEADS, SSD_P, SSD_N), jnp.float32)
    yc_s, Sf, Sb = ssd_stream(pc[..., :SSD_COLS], z_s, z_s, *ssd_p)
    y_s, _, _ = ssd_stream(p[..., :SSD_COLS], Sf, Sb, *ssd_p)
    z_l = jnp.zeros((B, LRU_WIDTH), jnp.float32)
    yc_l, hf, hb = lru_stream(pc[..., SSD_COLS:], z_l, z_l, *lru_p)
    y_l, _, _ = lru_stream(p[..., SSD_COLS:], hf, hb, *lru_p)
    return cols_to_grid(jnp.concatenate([y_s, y_l], -1)), jnp.concatenate([yc_s, yc_l], -1)


def peer(x, wq, keys, u_tab, v_tab):
    B, T, D = x.shape
    M = B * T
    xf = x.reshape(M, D)
    idx, gate = peer_route(xf, wq, keys)
    return peer_experts(xf, idx, gate, u_tab, v_tab).reshape(B, T, D)


def kernel(x, c, ctx, c_ctx, ada_w, ada_b, norm1_g, norm2_g, ev_w_in, rw_mu, rw_w0, rw_w_up, rw_a0, rw_a_up,
           rw_g_up, rw_k_k, rw_k_a, rw_r_k, rw_ln_g, rw_ln_b, gla_gate_up, gla_gate_b, gla_norm_g, od_w_in,
           ssd_conv_w, ssd_conv_b, ssd_dt_bias, ssd_A_log, ssd_D, ssd_norm_g, lru_conv_w, lru_conv_b, lru_wa,
           lru_ba, lru_wx, lru_bx, lru_lam, w_out, peer_wq, peer_keys, peer_u, peer_v, final_g):
    h, hc = x, ctx
    sc, scc = jax.nn.silu(c), jax.nn.silu(c_ctx)
    for l in range(DEPTH):
        last = l == DEPTH - 1
        i = l // 2
        mod = jnp.split((sc @ ada_w[l] + ada_b[l])[:, None, :], 6, axis=-1)
        modc = jnp.split(scc @ ada_w[l] + ada_b[l], 6, axis=-1)
        u = modulate(h, norm1_g[l], mod[0], mod[1])
        uc = modulate(hc, norm1_g[l], modc[0], modc[1])
        if l % 2 == 0:
            rw_p = (rw_w0[i], rw_w_up[i], rw_a0[i], rw_a_up[i], rw_g_up[i], rw_k_k[i], rw_k_a[i], rw_r_k[i],
                    rw_ln_g[i], rw_ln_b[i])
            gla_p = (gla_gate_up[i], gla_gate_b[i], gla_norm_g[i])
            y, yc = even_mixer(u, uc, ev_w_in[i], rw_mu[i], rw_p, gla_p)
        else:
            ssd_p = (ssd_conv_w[i], ssd_conv_b[i], ssd_dt_bias[i], ssd_A_log[i], ssd_D[i], ssd_norm_g[i])
            lru_p = (lru_conv_w[i], lru_conv_b[i], lru_wa[i], lru_ba[i], lru_wx[i], lru_bx[i], lru_lam[i])
            y, yc = odd_mixer(u, uc, od_w_in[i], ssd_p, lru_p)
        h = h + mod[2] * mm3(y, w_out[l])
        u = modulate(h, norm2_g[l], mod[3], mod[4])
        h = h + mod[5] * peer(u, peer_wq[l], peer_keys[l], peer_u[l], peer_v[l])
        if not last:
            hc = hc + modc[2] * mm3(yc, w_out[l])
            uc = modulate(hc, norm2_g[l], modc[3], modc[4])
            hc = hc + modc[5] * peer(uc, peer_wq[l], peer_keys[l], peer_u[l], peer_v[l])
    return rmsnorm(h, final_g)
```

```python
import functools

import jax
import jax.numpy as jnp
from jax import lax
import numpy as np
from jax.experimental import pallas as pl
from jax.experimental.pallas import tpu as pltpu

D_MODEL = 1024
DEPTH = 4
GRID_W = 64
MIX_WIDTH = D_MODEL

RW_WIDTH = MIX_WIDTH // 2
RW_HD = 64
RW_HEADS = RW_WIDTH // RW_HD
RW_DECAY_RANK = 32
RW_A_RANK = 32
RW_GATE_RANK = 64
RW_GN_EPS = 64e-5
RW_COLS = 3 * RW_WIDTH + RW_DECAY_RANK + RW_A_RANK + RW_GATE_RANK
GLA_HEADS = 4
GLA_DV = (MIX_WIDTH - RW_WIDTH) // GLA_HEADS
GLA_DK = GLA_DV // 2
GLA_GATE_RANK = 16
GLA_GATE_NORM = 16.0
GLA_CHUNK = 64
SSD_INNER = MIX_WIDTH // 2
SSD_P = 64
SSD_HEADS = SSD_INNER // SSD_P
SSD_N = 128
SSD_GROUPS = 2
SSD_CHUNK = 128
SSD_CONV_CH = SSD_INNER + 2 * SSD_GROUPS * SSD_N
SSD_COLS = SSD_INNER + SSD_CONV_CH + SSD_HEADS
LRU_WIDTH = MIX_WIDTH - SSD_INNER
LRU_BLOCKS = 8
LRU_BD = LRU_WIDTH // LRU_BLOCKS
LRU_C = 8.0
CONV_W = 4
PEER_HEADS = 8
N_KEYS = 128
PEER_TOPK = 16
PEER_DQ = 256
PEER_BLOCK = 128

VMEM_LIMIT_BYTES = 56 * 1024 * 1024


def _mm_kernel(a_ref, b_ref, o_ref):
    o_ref[...] = jnp.dot(a_ref[...].astype(jnp.bfloat16), b_ref[...].astype(jnp.bfloat16),
                         preferred_element_type=jnp.float32)


def pmm(a, b, tm=256):
    M, K = a.shape
    N = b.shape[1]
    assert M % tm == 0
    return pl.pallas_call(
        _mm_kernel,
        grid=(M // tm,),
        in_specs=[pl.BlockSpec((tm, K), lambda i: (i, 0)), pl.BlockSpec((K, N), lambda i: (0, 0))],
        out_specs=pl.BlockSpec((tm, N), lambda i: (i, 0)),
        out_shape=jax.ShapeDtypeStruct((M, N), jnp.float32),
        compiler_params=pltpu.CompilerParams(dimension_semantics=("arbitrary",),
                                             vmem_limit_bytes=VMEM_LIMIT_BYTES),
        name="pmm",
    )(a, b)


SUBLANES = 8
LANES = 128
PEER_E = PEER_HEADS * PEER_TOPK
PEER_TB = 8


def _fold8(p):
    sub = lax.broadcasted_iota(jnp.int32, (SUBLANES, LANES), 0)

    def comb(a, b, k):
        fa = a + pltpu.roll(a, SUBLANES - k, 0)
        fb = b + pltpu.roll(b, k, 0)
        return jnp.where((sub & k) == 0, fa, fb)

    return comb(comb(comb(p[0], p[4], 4), comb(p[2], p[6], 4), 2),
                comb(comb(p[1], p[5], 4), comb(p[3], p[7], 4), 2), 1)


def _peer_kernel(idx_cur_ref, idx_nxt_ref, gate_ref, x_ref, tab_ref, o_ref, buf, sem):
    i = pl.program_id(0)
    n = pl.num_programs(0)
    slot = i % 2
    nslot = 1 - slot

    def issue(idx_ref, t, s, e0=0, e1=PEER_E):
        for e in range(e0, e1):
            j = t * PEER_E + e
            pltpu.make_async_copy(tab_ref.at[idx_ref[0, 0, j]], buf.at[s, j], sem.at[s]).start(priority=e % 2)

    def wait_slot(s):
        pltpu.make_async_copy(buf.at[s], buf.at[s], sem.at[s]).wait()

    @pl.when(i == 0)
    def _():
        lax.fori_loop(0, PEER_TB, lambda t, c: (issue(idx_cur_ref, t, 0), c)[1], 0)

    wait_slot(slot)

    n_groups = PEER_E // SUBLANES
    per_group = PEER_E // (2 * n_groups)
    n_acc = 4

    def token(t, carry):
        xt = x_ref[t]
        base = t * PEER_E
        tok = (i % (ROUTE_TM // PEER_TB)) * PEER_TB + t
        onehot = (lax.broadcasted_iota(jnp.int32, (SUBLANES, ROUTE_TM), 1) == tok).astype(jnp.float32)
        coefs = []
        for g in range(n_groups):
            rows = [buf[slot, base + g * SUBLANES + j, 0] for j in range(SUBLANES)]
            issue(idx_nxt_ref, t, nslot, g * per_group, (g + 1) * per_group)
            a = jnp.sum(_fold8([r * xt for r in rows]), axis=1, keepdims=True)
            gate = jnp.sum(gate_ref[0, pl.ds(g * SUBLANES, SUBLANES), :] * onehot, axis=1, keepdims=True)
            coefs.append(jnp.broadcast_to(jax.nn.gelu(a) * gate, (SUBLANES, LANES)))
        accs = [jnp.zeros((SUBLANES, LANES), jnp.float32) for _ in range(n_acc)]
        for g in range(n_groups):
            rows = [buf[slot, base + g * SUBLANES + j, 1] for j in range(SUBLANES)]
            issue(idx_nxt_ref, t, nslot, (n_groups + g) * per_group, (n_groups + g + 1) * per_group)
            for j in range(SUBLANES):
                accs[j % n_acc] = accs[j % n_acc] + coefs[g][j:j + 1, :] * rows[j]
        o_ref[t] = (accs[0] + accs[1]) + (accs[2] + accs[3])
        return carry

    lax.fori_loop(0, PEER_TB, token, 0)

    @pl.when(i == n - 1)
    def _():
        wait_slot(nslot)


def peer_experts(xf, idx, gate_t, u_tab, v_tab):
    M, D = xf.shape
    NE = u_tab.shape[0]
    assert D == SUBLANES * LANES and M % ROUTE_TM == 0 and ROUTE_TM % PEER_TB == 0 and idx.shape == (M, PEER_E)
    assert gate_t.shape == (M // ROUTE_TM, PEER_E, ROUTE_TM)
    nblk = M // PEER_TB
    tab = jnp.stack([u_tab.reshape(NE, SUBLANES, LANES), v_tab.reshape(NE, SUBLANES, LANES)], axis=1)
    idx3 = idx.astype(jnp.int32).reshape(nblk, 1, PEER_TB * PEER_E)
    x3 = xf.reshape(M, SUBLANES, LANES)
    smem_blk = (1, 1, PEER_TB * PEER_E)
    out = pl.pallas_call(
        _peer_kernel,
        grid=(nblk,),
        in_specs=[
            pl.BlockSpec(smem_blk, lambda i: (i, 0, 0), memory_space=pltpu.SMEM),
            pl.BlockSpec(smem_blk, lambda i: (jnp.minimum(i + 1, nblk - 1), 0, 0), memory_space=pltpu.SMEM),
            pl.BlockSpec((1, PEER_E, ROUTE_TM), lambda i: (i // (ROUTE_TM // PEER_TB), 0, 0)),
            pl.BlockSpec((PEER_TB, SUBLANES, LANES), lambda i: (i, 0, 0)),
            pl.BlockSpec(memory_space=pl.ANY),
        ],
        out_specs=pl.BlockSpec((PEER_TB, SUBLANES, LANES), lambda i: (i, 0, 0)),
        out_shape=jax.ShapeDtypeStruct((M, SUBLANES, LANES), jnp.float32),
        scratch_shapes=[
            pltpu.VMEM((2, PEER_TB * PEER_E, 2, SUBLANES, LANES), jnp.float32),
            pltpu.SemaphoreType.DMA((2,)),
        ],
        compiler_params=pltpu.CompilerParams(dimension_semantics=("arbitrary",),
                                             vmem_limit_bytes=VMEM_LIMIT_BYTES),
        name="peer_experts",
    )(idx3, idx3, gate_t.astype(jnp.float32), x3, tab)
    return out.reshape(M, D)


ROUTE_TM = 128
PEER_HALF = PEER_DQ // 2


def _topk_rows(s, payload, k):
    n = s.shape[0]
    rid = lax.broadcasted_iota(jnp.int32, s.shape, 0)
    vals, sel = [], []
    for _ in range(k):
        m = jnp.max(s, axis=0, keepdims=True)
        pos = jnp.min(jnp.where(s == m, rid, n), axis=0, keepdims=True)
        hit = rid == pos
        vals.append(m)
        sel.append(pos if payload is None else jnp.max(jnp.where(hit, payload, -1), axis=0, keepdims=True))
        s = jnp.where(hit, -jnp.inf, s)
    return jnp.concatenate(vals, axis=0), jnp.concatenate(sel, axis=0)


def _route_kernel(x_ref, wq_ref, keys_ref, idx_ref, gate_ref):
    q = jnp.dot(x_ref[...].astype(jnp.bfloat16), wq_ref[...], preferred_element_type=jnp.float32)
    for h in range(PEER_HEADS):
        top_s, top_i = [], []
        for half in range(2):
            c0 = (h * 2 + half) * PEER_HALF
            qh = q[:, c0:c0 + PEER_HALF].astype(jnp.bfloat16)
            s = lax.dot_general(keys_ref[h, half], qh, (((1,), (1,)), ((), ())),
                                preferred_element_type=jnp.float32)
            ts, ti = _topk_rows(s, None, PEER_TOPK)
            top_s.append(ts)
            top_i.append(ti)
        cand_s = jnp.concatenate([top_s[0][i:i + 1, :] + top_s[1] for i in range(PEER_TOPK)], axis=0)
        cand_i = jnp.concatenate([top_i[0][i:i + 1, :] * N_KEYS + top_i[1] for i in range(PEER_TOPK)], axis=0)
        best_s, best_i = _topk_rows(cand_s, cand_i, PEER_TOPK)
        e = jnp.exp(best_s - best_s[0:1, :])
        r0 = h * PEER_TOPK
        gate_ref[0, r0:r0 + PEER_TOPK, :] = e / jnp.sum(e, axis=0, keepdims=True)
        idx_ref[0, r0:r0 + PEER_TOPK, :] = best_i


def peer_route(xf, wq, keys):
    M, D = xf.shape
    assert M % ROUTE_TM == 0
    nblk = M // ROUTE_TM
    out_blk = pl.BlockSpec((1, PEER_E, ROUTE_TM), lambda i: (i, 0, 0))
    idx, gate = pl.pallas_call(
        _route_kernel,
        grid=(nblk,),
        in_specs=[pl.BlockSpec((ROUTE_TM, D), lambda i: (i, 0)),
                  pl.BlockSpec(wq.shape, lambda i: (0, 0)),
                  pl.BlockSpec(keys.shape, lambda i: (0, 0, 0, 0))],
        out_specs=[out_blk, out_blk],
        out_shape=[jax.ShapeDtypeStruct((nblk, PEER_E, ROUTE_TM), jnp.int32),
                   jax.ShapeDtypeStruct((nblk, PEER_E, ROUTE_TM), jnp.float32)],
        compiler_params=pltpu.CompilerParams(dimension_semantics=("arbitrary",),
                                             vmem_limit_bytes=VMEM_LIMIT_BYTES),
        name="peer_route",
    )(xf, wq.astype(jnp.bfloat16), keys.astype(jnp.bfloat16))
    return idx.transpose(0, 2, 1).reshape(M, PEER_E), gate


def mm3(x, w):
    B, T, K = x.shape
    return pmm(x.reshape(B * T, K), w).reshape(B, T, w.shape[1])


def rmsnorm(x, g, eps=1e-6):
    xf = x.astype(jnp.float32)
    y = xf * lax.rsqrt(jnp.mean(xf * xf, -1, keepdims=True) + eps)
    return (y * g.astype(jnp.float32)).astype(x.dtype)


def modulate(x, g, shift, scale):
    return rmsnorm(x, g) * (1 + scale) + shift


def split_cols(p, sizes):
    return jnp.split(p, [int(i) for i in np.cumsum(sizes)], axis=-1)


def flip(t):
    return jnp.flip(t, axis=1)


def grid_to_cols(x):
    B, T, C = x.shape
    rows = T // GRID_W
    return x.reshape(B, rows, GRID_W, C).transpose(0, 2, 1, 3).reshape(B, T, C)


def cols_to_grid(x):
    B, T, C = x.shape
    rows = T // GRID_W
    return x.reshape(B, GRID_W, rows, C).transpose(0, 2, 1, 3).reshape(B, T, C)


def qshift_grid(x):
    B, T, C = x.shape
    rows = T // GRID_W
    g = jnp.pad(x.reshape(B, rows, GRID_W, C), ((0, 0), (1, 1), (1, 1), (0, 0)))
    sel = jnp.arange(C) % 4
    out = jnp.where(sel == 0, g[:, 1:-1, :-2], jnp.where(sel == 1, g[:, 1:-1, 2:],
                    jnp.where(sel == 2, g[:, :-2, 1:-1], g[:, 2:, 1:-1])))
    return out.reshape(B, T, C)


def shift_seq(x):
    prev = jnp.pad(x, ((0, 0), (1, 0), (0, 0)))[:, :-1]
    nxt = jnp.pad(x, ((0, 0), (0, 1), (0, 0)))[:, 1:]
    return jnp.where(jnp.arange(x.shape[-1]) % 2 == 0, prev, nxt)


def dwconv(x, w, b):
    lo, hi = (CONV_W - 1) // 2, CONV_W // 2
    T = x.shape[1]
    xp = jnp.pad(x, ((0, 0), (lo, hi), (0, 0)))
    y = sum(xp[:, j:j + T] * w[j].astype(x.dtype) for j in range(CONV_W))
    return y + b


RW_L = 64
RW_SUB = 16
RW_PAIRS = RW_HEADS // 2


def _bdot(a, b):
    return jnp.dot(a.astype(jnp.bfloat16), b.astype(jnp.bfloat16), preferred_element_type=jnp.float32)


def _bdot_nt(a, b):
    return lax.dot_general(a.astype(jnp.bfloat16), b.astype(jnp.bfloat16), (((1,), (1,)), ((), ())),
                           preferred_element_type=jnp.float32)


def _rwkv_chunk(lws, rs, ks, vs, kks, as_, states, reverse):
    L, L2 = RW_L, 2 * RW_L
    P = range(len(lws))
    row = lax.broadcasted_iota(jnp.int32, (L, L), 0)
    col = lax.broadcasted_iota(jnp.int32, (L, L), 1)
    before = (lambda a, b: a > b) if reverse else (lambda a, b: a < b)
    tri = jnp.logical_not(before(row, col)).astype(jnp.float32)
    row2 = lax.broadcasted_iota(jnp.int32, (L2, L2), 0)
    col2 = lax.broadcasted_iota(jnp.int32, (L2, L2), 1)
    same = (row2 // L) == (col2 // L)
    strict = same & before(col2, row2)
    incl = same & jnp.logical_not(before(row2, col2))
    diag_blk = (row2 // RW_SUB) == (col2 // RW_SUB)
    eye = (row2 == col2).astype(jnp.float32)
    head_a = (lax.broadcasted_iota(jnp.int32, (1, LANES), 1) // RW_HD) == 0

    def stack2(x):
        return jnp.concatenate([jnp.where(head_a, x, 0.0), jnp.where(head_a, 0.0, x)], axis=0)

    cum = [jnp.dot(tri, lws[p], precision=lax.Precision.HIGHEST, preferred_element_type=jnp.float32) for p in P]
    last = 0 if reverse else L - 1
    cum_l = [cum[p][last:last + 1, :] for p in P]
    b = [as_[p] * kks[p] for p in P]
    ginv = [jnp.exp(-cum[p]) for p in P]
    gtail = [jnp.exp(cum_l[p] - cum[p]) for p in P]
    left = [jnp.concatenate([stack2(kks[p] * jnp.exp(cum[p] - lws[p])), stack2(rs[p] * jnp.exp(cum[p]))], axis=0)
            for p in P]
    right = [jnp.concatenate([stack2(b[p] * ginv[p]), stack2(ks[p] * ginv[p])], axis=0) for p in P]
    v2 = [stack2(vs[p]) for p in P]
    g = [_bdot_nt(left[p], right[p]) for p in P]
    wy = [_bdot_nt(left[p], states[p]) for p in P]
    n = [jnp.where(strict, g[p][:L2, :L2], 0.0) for p in P]
    mkk = [jnp.where(strict, g[p][:L2, L2:], 0.0) for p in P]
    mrb = [jnp.where(incl, g[p][L2:, :L2], 0.0) for p in P]
    mrk = [jnp.where(incl, g[p][L2:, L2:], 0.0) for p in P]
    d = [jnp.where(diag_blk, n[p], 0.0) for p in P]
    d2 = [_bdot(d[p], d[p]) for p in P]
    d4 = [_bdot(d2[p], d2[p]) for p in P]
    d8 = [_bdot(d4[p], d4[p]) for p in P]
    tinv = [_bdot(eye - d[p], eye + d2[p]) for p in P]
    tinv = [_bdot(tinv[p], eye + d4[p]) for p in P]
    tinv = [_bdot(tinv[p], eye + d8[p]) for p in P]
    rhs = [wy[p][:L2] + _bdot(mkk[p], v2[p]) for p in P]

    nblk = L // RW_SUB
    zero_blk = jnp.zeros((RW_SUB, LANES), jnp.float32)

    def pick(x, blk):
        lo = blk * RW_SUB
        return jnp.concatenate([x[lo:lo + RW_SUB], x[L + lo:L + lo + RW_SUB]], axis=0)

    def place(blocks):
        half = lambda h: [zero_blk if blocks[j] is None else blocks[j][h * RW_SUB:(h + 1) * RW_SUB]
                          for j in range(nblk)]
        return jnp.concatenate(half(0) + half(1), axis=0)

    u_blocks = [[None] * nblk for _ in P]
    order = list(reversed(range(nblk))) if reverse else list(range(nblk))
    for blk in order:
        for p in P:
            acc = pick(rhs[p], blk)
            if blk != order[0]:
                acc = acc - _bdot(pick(n[p], blk), place(u_blocks[p]))
            cur = [None] * nblk
            cur[blk] = acc
            u_blocks[p][blk] = _bdot(pick(tinv[p], blk), place(cur))
    u = [place(u_blocks[p]) for p in P]
    y2 = [wy[p][L2:] - _bdot(mrb[p], u[p]) + _bdot(mrk[p], v2[p]) for p in P]
    ys = [y2[p][:L] + y2[p][L:] for p in P]
    upd = [_bdot(v2[p].T, stack2(ks[p] * gtail[p])) - _bdot(u[p].T, stack2(b[p] * gtail[p])) for p in P]
    s_news = [states[p] * jnp.exp(cum_l[p]) + upd[p] for p in P]
    return ys, s_news


def _rwkv_kernel(lw_ref, r_ref, k_ref, v_ref, kk_ref, a_ref, s0_ref, y_ref, sT_ref, s_scr, *, reverse):
    c = pl.program_id(1)

    @pl.when(c == 0)
    def _():
        s_scr[...] = s0_ref[0]

    slabs = [pl.ds(p * LANES, LANES) for p in range(RW_PAIRS)]
    ys, s_news = _rwkv_chunk(*[[ref[0, :, sl] for sl in slabs]
                               for ref in (lw_ref, r_ref, k_ref, v_ref, kk_ref, a_ref)],
                             [s_scr[p] for p in range(RW_PAIRS)], reverse)
    for p in range(RW_PAIRS):
        y_ref[0, :, slabs[p]] = ys[p]
        s_scr[p] = s_news[p]

    @pl.when(c == pl.num_programs(1) - 1)
    def _():
        sT_ref[0] = s_scr[...]


def rwkv7_scan(r, lw, k, v, kk, a, S0, reverse=False):
    B, T, C = r.shape
    assert C == RW_PAIRS * LANES and T % RW_L == 0
    nc = T // RW_L
    seq = pl.BlockSpec((1, RW_L, C), (lambda b, c: (b, nc - 1 - c, 0)) if reverse else (lambda b, c: (b, c, 0)))
    st = pl.BlockSpec((1, RW_PAIRS, LANES, LANES), lambda b, c: (b, 0, 0, 0))
    y, sT = pl.pallas_call(
        functools.partial(_rwkv_kernel, reverse=reverse),
        grid=(B, nc),
        in_specs=[seq] * 6 + [st],
        out_specs=[seq, st],
        out_shape=[jax.ShapeDtypeStruct((B, T, C), jnp.float32),
                   jax.ShapeDtypeStruct((B, RW_PAIRS, LANES, LANES), jnp.float32)],
        scratch_shapes=[pltpu.VMEM((RW_PAIRS, LANES, LANES), jnp.float32)],
        compiler_params=pltpu.CompilerParams(dimension_semantics=("arbitrary", "arbitrary"),
                                             vmem_limit_bytes=VMEM_LIMIT_BYTES),
        name="rwkv7_scan",
    )(lw, r, k, v, kk, a, S0)
    return y, sT


def rwkv_stream(p, S0, w0, w_up, a0, a_up, g_up, k_k, k_a, r_k, ln_g, ln_b):
    B, T, _ = p.shape
    r, k, v, wd, ad, gd = split_cols(p, [RW_WIDTH, RW_WIDTH, RW_WIDTH, RW_DECAY_RANK, RW_A_RANK])
    heads = lambda t: t.reshape(B, T, RW_HEADS, RW_HD)
    kk = heads(k * k_k).astype(jnp.float32)
    kk = (kk * lax.rsqrt(jnp.maximum(jnp.sum(kk * kk, -1, keepdims=True), 1e-12))).reshape(B, T, RW_WIDTH)
    g = jax.nn.sigmoid(gd) @ g_up
    per_dir, bonus = [], []
    for d in range(2):
        wl = (w0[d] + jnp.tanh(wd) @ w_up[d]).astype(jnp.float32)
        lw = -jnp.exp(-jax.nn.softplus(-wl) - 0.5)
        a = jax.nn.sigmoid(a0[d] + ad @ a_up[d])
        kd = k * (1 + (a - 1) * k_a)
        per_dir.append(rwkv7_scan(r, lw, kd, v, kk, a, S0[d], reverse=(d == 1)))
        bonus.append(jnp.sum(heads(r) * heads(kd) * r_k, -1, keepdims=True) * heads(v))
    sT = (per_dir[0][1], per_dir[1][1])
    y = (per_dir[0][0] + per_dir[1][0]).astype(jnp.float32).reshape(B, T, RW_HEADS, RW_HD)
    mu = jnp.mean(y, -1, keepdims=True)
    var = jnp.mean(jnp.square(y - mu), -1, keepdims=True)
    yn = ((y - mu) * lax.rsqrt(var + RW_GN_EPS)).reshape(B, T, RW_WIDTH) * ln_g + ln_b
    out = (yn + (bonus[0] + bonus[1]).reshape(B, T, RW_WIDTH)) * g
    return out.astype(p.dtype), sT


GLA_SUB = 16
GLA_PAIRS = GLA_HEADS // 2
assert GLA_DV == LANES and 2 * GLA_DK == LANES


def _gla_kernel(q_ref, k_ref, g_ref, v_ref, s0_ref, o_ref, sT_ref, s_scr, *, reverse):
    c = pl.program_id(1)
    L = GLA_CHUNK

    @pl.when(c == 0)
    def _():
        s_scr[...] = s0_ref[0]

    row = lax.broadcasted_iota(jnp.int32, (L, L), 0)
    col = lax.broadcasted_iota(jnp.int32, (L, L), 1)
    incl = (col >= row) if reverse else (col <= row)
    last = 0 if reverse else L - 1
    lane = lax.broadcasted_iota(jnp.int32, (1, LANES), 1)
    trow = lax.broadcasted_iota(jnp.int32, (L, LANES), 0)
    outs = []
    for p in range(GLA_PAIRS):
        sl = pl.ds(p * LANES, LANES)
        q, k, g = q_ref[0, :, sl], k_ref[0, :, sl], g_ref[0, :, sl]
        b = jnp.dot(incl.astype(jnp.float32), g, precision=lax.Precision.HIGHEST,
                    preferred_element_type=jnp.float32)
        b_l = b[last:last + 1, :]
        qe = q * jnp.exp(b)
        kd = k * jnp.exp(b_l - b)
        qfs, kfs = [], []
        for blk in range(L // GLA_SUB):
            lo, hi = blk * GLA_SUB, (blk + 1) * GLA_SUB
            first = hi - 1 if reverse else lo
            b_ref = b[first:first + 1, :]
            qfs.append(q[lo:hi, :] * jnp.exp(b[lo:hi, :] - b_ref))
            used = (trow >= lo) if reverse else (trow < hi)
            kfs.append(jnp.where(used, k * jnp.exp(jnp.where(used, b_ref - b, 0.0)), 0.0))
        for hh in range(2):
            h = 2 * p + hh
            m = (lane // GLA_DK) == hh
            v = v_ref[0, :, pl.ds(h * GLA_DV, GLA_DV)]
            s_t = s_scr[h]
            a = jnp.concatenate([_bdot_nt(qf, jnp.where(m, kf, 0.0)) for qf, kf in zip(qfs, kfs)], axis=0)
            o = _bdot(jnp.where(incl, a, 0.0), v) + _bdot_nt(qe, s_t)
            s_new = s_t * jnp.exp(b_l) + _bdot(v.T, jnp.where(m, kd, 0.0))
            outs.append((h, o, s_new))
    for h, o, s_new in outs:
        o_ref[0, :, pl.ds(h * GLA_DV, GLA_DV)] = o
        s_scr[h] = s_new

    @pl.when(c == pl.num_programs(1) - 1)
    def _():
        sT_ref[0] = s_scr[...]


def gla_chunked(q, k, g, v, S0, reverse=False):
    B, T, _ = q.shape
    assert T % GLA_CHUNK == 0
    nc = T // GLA_CHUNK
    chunk = (lambda b, c: (b, nc - 1 - c, 0)) if reverse else (lambda b, c: (b, c, 0))
    qk = pl.BlockSpec((1, GLA_CHUNK, GLA_HEADS * GLA_DK), chunk)
    vv = pl.BlockSpec((1, GLA_CHUNK, GLA_HEADS * GLA_DV), chunk)
    st = pl.BlockSpec((1, GLA_HEADS, LANES, LANES), lambda b, c: (b, 0, 0, 0))
    return pl.pallas_call(
        functools.partial(_gla_kernel, reverse=reverse),
        grid=(B, nc),
        in_specs=[qk, qk, qk, vv, st],
        out_specs=[vv, st],
        out_shape=[jax.ShapeDtypeStruct((B, T, GLA_HEADS * GLA_DV), jnp.float32),
                   jax.ShapeDtypeStruct((B, GLA_HEADS, LANES, LANES), jnp.float32)],
        scratch_shapes=[pltpu.VMEM((GLA_HEADS, LANES, LANES), jnp.float32)],
        compiler_params=pltpu.CompilerParams(dimension_semantics=("arbitrary", "arbitrary"),
                                             vmem_limit_bytes=VMEM_LIMIT_BYTES),
        name="gla_chunked",
    )(q, k, g, v, S0)


def gla_stream(p, S0, gate_up, gate_b, norm_g):
    B, T, _ = p.shape
    q, k, v, gd, og = split_cols(p, [GLA_HEADS * GLA_DK, GLA_HEADS * GLA_DK, GLA_HEADS * GLA_DV, GLA_GATE_RANK])
    q = q * GLA_DK ** -0.5
    outs, states = [], []
    for d in range(2):
        glog = jax.nn.log_sigmoid((gd @ gate_up[d] + gate_b[d]).astype(jnp.float32)) / GLA_GATE_NORM
        od, sd = gla_chunked(q, k, glog, v, S0[d], reverse=(d == 1))
        outs.append(od)
        states.append(sd)
    o = (outs[0] + outs[1]).reshape(B, T, GLA_HEADS, GLA_DV)
    o = rmsnorm(o, norm_g.reshape(GLA_HEADS, GLA_DV)).reshape(B, T, GLA_HEADS * GLA_DV)
    return (o * jax.nn.silu(og)).astype(p.dtype), tuple(states)


def ssd_chunked(x, dt, A, Bm, Cm, S0):
    Bsz, T, H, P = x.shape
    N = Bm.shape[-1]
    L = SSD_CHUNK
    n = T // L
    f = lambda t: t.astype(jnp.float32)
    xc = (f(x) * f(dt)[..., None]).reshape(Bsz, n, L, H, P)
    Bc = f(Bm).reshape(Bsz, n, L, H, N)
    Cc = f(Cm).reshape(Bsz, n, L, H, N)
    acs = jnp.cumsum((f(dt) * A).reshape(Bsz, n, L, H).transpose(0, 3, 1, 2), -1)
    mask = jnp.tril(jnp.ones((L, L), bool))
    Lmat = jnp.exp(jnp.where(mask, acs[..., :, None] - acs[..., None, :], -jnp.inf))
    G = jnp.einsum("bclhn,bcshn->bhcls", Cc, Bc) * Lmat
    y_diag = jnp.einsum("bhcls,bcshp->bclhp", G, xc)
    contrib = jnp.einsum("bclhn,bhcl,bclhp->bchpn", Bc, jnp.exp(acs[..., -1:] - acs), xc)
    chunk_decay = jnp.exp(acs[..., -1])

    def step(S, inp):
        dec, st = inp
        return dec[..., None, None] * S + st, S

    S, S_in = lax.scan(step, S0, (jnp.moveaxis(chunk_decay, 2, 0), jnp.moveaxis(contrib, 1, 0)))
    y_off = jnp.einsum("bclhn,bchpn,bhcl->bclhp", Cc, jnp.moveaxis(S_in, 0, 1), jnp.exp(acs))
    return (y_diag + y_off).reshape(Bsz, T, H, P).astype(x.dtype), S


def ssd_stream(p, S0f, S0b, conv_w, conv_b, dt_bias, A_log, D_skip, norm_g):
    B, T, _ = p.shape
    z, xbc, dtr = split_cols(p, [SSD_INNER, SSD_CONV_CH])
    xbc = jax.nn.silu(dwconv(xbc, conv_w, conv_b))
    xs, Bm, Cm = split_cols(xbc, [SSD_INNER, SSD_GROUPS * SSD_N])
    xs = xs.reshape(B, T, SSD_HEADS, SSD_P)
    rep = SSD_HEADS // SSD_GROUPS
    Bm = jnp.repeat(Bm.reshape(B, T, SSD_GROUPS, SSD_N), rep, axis=2)
    Cm = jnp.repeat(Cm.reshape(B, T, SSD_GROUPS, SSD_N), rep, axis=2)
    ys, states = [], []
    for d, S0 in enumerate((S0f, S0b)):
        dt = jax.nn.softplus((dtr + dt_bias[d]).astype(jnp.float32))
        A = -jnp.exp(A_log[d].astype(jnp.float32))
        args = [xs, dt, Bm, Cm]
        if d == 1:
            args = [flip(t) for t in args]
        yd, Sd = ssd_chunked(args[0], args[1], A, args[2], args[3], S0)
        ys.append(flip(yd) if d == 1 else yd)
        states.append(Sd)
    y = (ys[0] + ys[1] + xs * D_skip[:, None]).reshape(B, T, SSD_INNER) * jax.nn.silu(z)
    return rmsnorm(y, norm_g).astype(p.dtype), states[0], states[1]


def linear_scan(a, b, h0):
    b = b.at[:, 0].add(a[:, 0] * h0)
    comb = lambda l, r: (l[0] * r[0], r[0] * l[1] + r[1])
    _, h = lax.associative_scan(comb, (a, b), axis=1)
    return h, h[:, -1]


def lru_stream(p, h0f, h0b, conv_w, conv_b, wa, ba, wx, bx, lam):
    B, T, _ = p.shape
    xr, gate = split_cols(p, [LRU_WIDTH])
    xf = dwconv(xr, conv_w, conv_b).astype(jnp.float32)
    xh = xf.reshape(B, T, LRU_BLOCKS, LRU_BD)
    hs, finals = [], []
    for d, h0 in enumerate((h0f, h0b)):
        r = jax.nn.sigmoid(jnp.einsum("btgi,gij->btgj", xh, wa[d]).reshape(B, T, LRU_WIDTH) + ba[d])
        i = jax.nn.sigmoid(jnp.einsum("btgi,gij->btgj", xh, wx[d]).reshape(B, T, LRU_WIDTH) + bx[d])
        log_a = -LRU_C * r * jax.nn.softplus(-lam[d])
        a = jnp.exp(log_a)
        b = jnp.sqrt(-jnp.expm1(2 * log_a)) * (i * xf)
        if d == 1:
            a, b = flip(a), flip(b)
        h, hT = linear_scan(a, b, h0)
        hs.append(flip(h) if d == 1 else h)
        finals.append(hT)
    y = (hs[0] + hs[1]).astype(p.dtype) * jax.nn.gelu(gate)
    return y, finals[0], finals[1]


def even_mixer(u, uc, w_in, mu, rw_p, gla_p):
    B = u.shape[0]
    p, pc = mm3(u, w_in), mm3(uc, w_in)
    rw, gl = p[..., :RW_COLS], p[..., RW_COLS:]
    rwc, glc = pc[..., :RW_COLS], pc[..., RW_COLS:]
    rw = rw + mu * (qshift_grid(rw) - rw)
    rwc = rwc + mu * (shift_seq(rwc) - rwc)
    z_rw = jnp.zeros((B, RW_PAIRS, LANES, LANES), jnp.float32)
    yc_rw, s_ctx = rwkv_stream(rwc, (z_rw, z_rw), *rw_p)
    y_rw, _ = rwkv_stream(rw, s_ctx, *rw_p)
    z_gla = jnp.zeros((B, GLA_HEADS, LANES, LANES), jnp.float32)
    yc_gl, g_ctx = gla_stream(glc, (z_gla, z_gla), *gla_p)
    y_gl, _ = gla_stream(gl, g_ctx, *gla_p)
    return jnp.concatenate([y_rw, y_gl], -1), jnp.concatenate([yc_rw, yc_gl], -1)


def odd_mixer(u, uc, w_in, ssd_p, lru_p):
    B = u.shape[0]
    p = grid_to_cols(mm3(u, w_in))
    pc = mm3(uc, w_in)
    z_s = jnp.zeros((B, SSD_HEADS, SSD_P, SSD_N), jnp.float32)
    yc_s, Sf, Sb = ssd_stream(pc[..., :SSD_COLS], z_s, z_s, *ssd_p)
    y_s, _, _ = ssd_stream(p[..., :SSD_COLS], Sf, Sb, *ssd_p)
    z_l = jnp.zeros((B, LRU_WIDTH), jnp.float32)
    yc_l, hf, hb = lru_stream(pc[..., SSD_COLS:], z_l, z_l, *lru_p)
    y_l, _, _ = lru_stream(p[..., SSD_COLS:], hf, hb, *lru_p)
    return cols_to_grid(jnp.concatenate([y_s, y_l], -1)), jnp.concatenate([yc_s, yc_l], -1)


def peer(x, wq, keys, u_tab, v_tab):
    B, T, D = x.shape
    M = B * T
    xf = x.reshape(M, D)
    idx, gate_t = peer_route(xf, wq, keys)
    return peer_experts(xf, idx, gate_t, u_tab, v_tab).reshape(B, T, D)


def kernel(x, c, ctx, c_ctx, ada_w, ada_b, norm1_g, norm2_g, ev_w_in, rw_mu, rw_w0, rw_w_up, rw_a0, rw_a_up,
           rw_g_up, rw_k_k, rw_k_a, rw_r_k, rw_ln_g, rw_ln_b, gla_gate_up, gla_gate_b, gla_norm_g, od_w_in,
           ssd_conv_w, ssd_conv_b, ssd_dt_bias, ssd_A_log, ssd_D, ssd_norm_g, lru_conv_w, lru_conv_b, lru_wa,
           lru_ba, lru_wx, lru_bx, lru_lam, w_out, peer_wq, peer_keys, peer_u, peer_v, final_g):
    h, hc = x, ctx
    sc, scc = jax.nn.silu(c), jax.nn.silu(c_ctx)
    for l in range(DEPTH):
        last = l == DEPTH - 1
        i = l // 2
        mod = jnp.split((sc @ ada_w[l] + ada_b[l])[:, None, :], 6, axis=-1)
        modc = jnp.split(scc @ ada_w[l] + ada_b[l], 6, axis=-1)
        u = modulate(h, norm1_g[l], mod[0], mod[1])
        uc = modulate(hc, norm1_g[l], modc[0], modc[1])
        if l % 2 == 0:
            rw_p = (rw_w0[i], rw_w_up[i], rw_a0[i], rw_a_up[i], rw_g_up[i], rw_k_k[i], rw_k_a[i], rw_r_k[i],
                    rw_ln_g[i], rw_ln_b[i])
            gla_p = (gla_gate_up[i], gla_gate_b[i], gla_norm_g[i])
            y, yc = even_mixer(u, uc, ev_w_in[i], rw_mu[i], rw_p, gla_p)
        else:
            ssd_p = (ssd_conv_w[i], ssd_conv_b[i], ssd_dt_bias[i], ssd_A_log[i], ssd_D[i], ssd_norm_g[i])
            lru_p = (lru_conv_w[i], lru_conv_b[i], lru_wa[i], lru_ba[i], lru_wx[i], lru_bx[i], lru_lam[i])
            y, yc = odd_mixer(u, uc, od_w_in[i], ssd_p, lru_p)
        h = h + mod[2] * mm3(y, w_out[l])
        u = modulate(h, norm2_g[l], mod[3], mod[4])
        h = h + mod[5] * peer(u, peer_wq[l], peer_keys[l], peer_u[l], peer_v[l])
        if not last:
            hc = hc + modc[2] * mm3(yc, w_out[l])
            uc = modulate(hc, norm2_g[l], modc[3], modc[4])
            hc = hc + modc[5] * peer(uc, peer_wq[l], peer_keys[l], peer_u[l], peer_v[l])
    return rmsnorm(h, final_g)
```

```python
import functools

import jax
import jax.numpy as jnp
from jax import lax
import numpy as np
from jax.experimental import pallas as pl
from jax.experimental.pallas import tpu as pltpu

D_MODEL = 1024
DEPTH = 4
GRID_W = 64
MIX_WIDTH = D_MODEL

RW_WIDTH = MIX_WIDTH // 2
RW_HD = 64
RW_HEADS = RW_WIDTH // RW_HD
RW_DECAY_RANK = 32
RW_A_RANK = 32
RW_GATE_RANK = 64
RW_GN_EPS = 64e-5
RW_COLS = 3 * RW_WIDTH + RW_DECAY_RANK + RW_A_RANK + RW_GATE_RANK
GLA_HEADS = 4
GLA_DV = (MIX_WIDTH - RW_WIDTH) // GLA_HEADS
GLA_DK = GLA_DV // 2
GLA_GATE_RANK = 16
GLA_GATE_NORM = 16.0
GLA_CHUNK = 64
SSD_INNER = MIX_WIDTH // 2
SSD_P = 64
SSD_HEADS = SSD_INNER // SSD_P
SSD_N = 128
SSD_GROUPS = 2
SSD_CHUNK = 128
SSD_CONV_CH = SSD_INNER + 2 * SSD_GROUPS * SSD_N
SSD_COLS = SSD_INNER + SSD_CONV_CH + SSD_HEADS
LRU_WIDTH = MIX_WIDTH - SSD_INNER
LRU_BLOCKS = 8
LRU_BD = LRU_WIDTH // LRU_BLOCKS
LRU_C = 8.0
CONV_W = 4
PEER_HEADS = 8
N_KEYS = 128
PEER_TOPK = 16
PEER_DQ = 256
PEER_BLOCK = 128

VMEM_LIMIT_BYTES = 56 * 1024 * 1024


def _mm_kernel(a_ref, b_ref, o_ref):
    o_ref[...] = jnp.dot(a_ref[...].astype(jnp.bfloat16), b_ref[...].astype(jnp.bfloat16),
                         preferred_element_type=jnp.float32)


def pmm(a, b, tm=256):
    M, K = a.shape
    N = b.shape[1]
    assert M % tm == 0
    return pl.pallas_call(
        _mm_kernel,
        grid=(M // tm,),
        in_specs=[pl.BlockSpec((tm, K), lambda i: (i, 0)), pl.BlockSpec((K, N), lambda i: (0, 0))],
        out_specs=pl.BlockSpec((tm, N), lambda i: (i, 0)),
        out_shape=jax.ShapeDtypeStruct((M, N), jnp.float32),
        compiler_params=pltpu.CompilerParams(dimension_semantics=("arbitrary",),
                                             vmem_limit_bytes=VMEM_LIMIT_BYTES),
        name="pmm",
    )(a, b)


SUBLANES = 8
LANES = 128
PEER_E = PEER_HEADS * PEER_TOPK
PEER_TB = 8
PEER_SLOTS = 3


def _fold8(p):
    sub = lax.broadcasted_iota(jnp.int32, (SUBLANES, LANES), 0)

    def comb(a, b, k):
        fa = a + pltpu.roll(a, SUBLANES - k, 0)
        fb = b + pltpu.roll(b, k, 0)
        return jnp.where((sub & k) == 0, fa, fb)

    return comb(comb(comb(p[0], p[4], 4), comb(p[2], p[6], 4), 2),
                comb(comb(p[1], p[5], 4), comb(p[3], p[7], 4), 2), 1)


def _peer_kernel(idx_cur_ref, idx_nx1_ref, idx_nxt_ref, gate_ref, x_ref, tab_ref, o_ref, buf, sem):
    i = pl.program_id(0)
    n = pl.num_programs(0)
    slot = i % PEER_SLOTS
    nslot = (i + PEER_SLOTS - 1) % PEER_SLOTS

    def issue(idx_ref, t, s, e0=0, e1=PEER_E):
        for e in range(e0, e1):
            j = t * PEER_E + e
            pltpu.make_async_copy(tab_ref.at[idx_ref[0, 0, j]], buf.at[s, j], sem.at[s]).start(priority=e % 2)

    def wait_slot(s):
        pltpu.make_async_copy(buf.at[s], buf.at[s], sem.at[s]).wait()

    @pl.when(i == 0)
    def _():
        lax.fori_loop(0, PEER_TB, lambda t, c: (issue(idx_cur_ref, t, 0), c)[1], 0)
        lax.fori_loop(0, PEER_TB, lambda t, c: (issue(idx_nx1_ref, t, 1), c)[1], 0)

    wait_slot(slot)

    n_groups = PEER_E // SUBLANES
    per_group = PEER_E // (2 * n_groups)
    n_acc = 4

    def token(t, carry):
        xt = x_ref[t]
        base = t * PEER_E
        tok = (i % (ROUTE_TM // PEER_TB)) * PEER_TB + t
        onehot = (lax.broadcasted_iota(jnp.int32, (SUBLANES, ROUTE_TM), 1) == tok).astype(jnp.float32)
        coefs = []
        for g in range(n_groups):
            rows = [buf[slot, base + g * SUBLANES + j, 0] for j in range(SUBLANES)]
            issue(idx_nxt_ref, t, nslot, g * per_group, (g + 1) * per_group)
            a = jnp.sum(_fold8([r * xt for r in rows]), axis=1, keepdims=True)
            gate = jnp.sum(gate_ref[0, pl.ds(g * SUBLANES, SUBLANES), :] * onehot, axis=1, keepdims=True)
            coefs.append(jnp.broadcast_to(jax.nn.gelu(a) * gate, (SUBLANES, LANES)))
        accs = [jnp.zeros((SUBLANES, LANES), jnp.float32) for _ in range(n_acc)]
        for g in range(n_groups):
            rows = [buf[slot, base + g * SUBLANES + j, 1] for j in range(SUBLANES)]
            issue(idx_nxt_ref, t, nslot, (n_groups + g) * per_group, (n_groups + g + 1) * per_group)
            for j in range(SUBLANES):
                accs[j % n_acc] = accs[j % n_acc] + coefs[g][j:j + 1, :] * rows[j]
        o_ref[t] = (accs[0] + accs[1]) + (accs[2] + accs[3])
        return carry

    lax.fori_loop(0, PEER_TB, token, 0)

    @pl.when(i == n - 1)
    def _():
        wait_slot((i + 1) % PEER_SLOTS)
        wait_slot(nslot)


def peer_experts(xf, idx, gate_t, u_tab, v_tab):
    M, D = xf.shape
    NE = u_tab.shape[0]
    assert D == SUBLANES * LANES and M % ROUTE_TM == 0 and ROUTE_TM % PEER_TB == 0 and idx.shape == (M, PEER_E)
    assert gate_t.shape == (M // ROUTE_TM, PEER_E, ROUTE_TM)
    nblk = M // PEER_TB
    tab = jnp.stack([u_tab.reshape(NE, SUBLANES, LANES), v_tab.reshape(NE, SUBLANES, LANES)], axis=1)
    idx3 = idx.astype(jnp.int32).reshape(nblk, 1, PEER_TB * PEER_E)
    x3 = xf.reshape(M, SUBLANES, LANES)
    smem_blk = (1, 1, PEER_TB * PEER_E)
    out = pl.pallas_call(
        _peer_kernel,
        grid=(nblk,),
        in_specs=[
            pl.BlockSpec(smem_blk, lambda i: (i, 0, 0), memory_space=pltpu.SMEM),
            pl.BlockSpec(smem_blk, lambda i: (jnp.minimum(i + 1, nblk - 1), 0, 0), memory_space=pltpu.SMEM),
            pl.BlockSpec(smem_blk, lambda i: (jnp.minimum(i + 2, nblk - 1), 0, 0), memory_space=pltpu.SMEM),
            pl.BlockSpec((1, PEER_E, ROUTE_TM), lambda i: (i // (ROUTE_TM // PEER_TB), 0, 0)),
            pl.BlockSpec((PEER_TB, SUBLANES, LANES), lambda i: (i, 0, 0)),
            pl.BlockSpec(memory_space=pl.ANY),
        ],
        out_specs=pl.BlockSpec((PEER_TB, SUBLANES, LANES), lambda i: (i, 0, 0)),
        out_shape=jax.ShapeDtypeStruct((M, SUBLANES, LANES), jnp.float32),
        scratch_shapes=[
            pltpu.VMEM((PEER_SLOTS, PEER_TB * PEER_E, 2, SUBLANES, LANES), jnp.float32),
            pltpu.SemaphoreType.DMA((PEER_SLOTS,)),
        ],
        compiler_params=pltpu.CompilerParams(dimension_semantics=("arbitrary",),
                                             vmem_limit_bytes=VMEM_LIMIT_BYTES),
        name="peer_experts",
    )(idx3, idx3, idx3, gate_t.astype(jnp.float32), x3, tab)
    return out.reshape(M, D)


ROUTE_TM = 128
PEER_HALF = PEER_DQ // 2


def _topk_rows(s, k, rank=None, payload=None):
    if rank is None:
        rank = lax.broadcasted_iota(jnp.int32, s.shape, 0)
    big = jnp.iinfo(jnp.int32).max
    vals, sel = [], []
    for _ in range(k):
        m = jnp.max(s, axis=0, keepdims=True)
        pos = jnp.min(jnp.where(s == m, rank, big), axis=0, keepdims=True)
        hit = rank == pos
        vals.append(m)
        sel.append(pos if payload is None else jnp.max(jnp.where(hit, payload, -1), axis=0, keepdims=True))
        s = jnp.where(hit, -jnp.inf, s)
    return jnp.concatenate(vals, axis=0), jnp.concatenate(sel, axis=0)


def _product_candidates(top_s, top_i):
    K = PEER_TOPK
    s1, s2 = top_s
    i1, i2 = top_i
    TM = s1.shape[1]
    vals, flat, eid = [], [], []
    for i in range(2):
        nj = K // (i + 1)
        j = lax.broadcasted_iota(jnp.int32, (nj, TM), 0)
        vals.append(s1[i:i + 1, :] + s2[:nj, :])
        flat.append(i * K + j)
        eid.append(i1[i:i + 1, :] * N_KEYS + i2[:nj, :])
    for j in range(K // 3):
        ni_end = K // (j + 1)
        rows = K if ni_end > SUBLANES else SUBLANES
        i = lax.broadcasted_iota(jnp.int32, (rows, TM), 0)
        ok = (i >= 2) & (i < ni_end)
        vals.append(jnp.where(ok, s1[:rows, :] + s2[j:j + 1, :], -jnp.inf))
        flat.append(i * K + j)
        eid.append(i1[:rows, :] * N_KEYS + i2[j:j + 1, :])
    cat = lambda xs: jnp.concatenate(xs, axis=0)
    return cat(vals), cat(flat), cat(eid)


def _route_kernel(x_ref, wq_ref, keys_ref, idx_ref, gate_ref):
    q = jnp.dot(x_ref[...].astype(jnp.bfloat16), wq_ref[...], preferred_element_type=jnp.float32)
    for h in range(PEER_HEADS):
        top_s, top_i = [], []
        for half in range(2):
            c0 = (h * 2 + half) * PEER_HALF
            qh = q[:, c0:c0 + PEER_HALF].astype(jnp.bfloat16)
            s = lax.dot_general(keys_ref[h, half], qh, (((1,), (1,)), ((), ())),
                                preferred_element_type=jnp.float32)
            ts, ti = _topk_rows(s, PEER_TOPK)
            top_s.append(ts)
            top_i.append(ti)
        cand_s, cand_flat, cand_i = _product_candidates(top_s, top_i)
        best_s, best_i = _topk_rows(cand_s, PEER_TOPK, cand_flat, cand_i)
        e = jnp.exp(best_s - best_s[0:1, :])
        r0 = h * PEER_TOPK
        gate_ref[0, r0:r0 + PEER_TOPK, :] = e / jnp.sum(e, axis=0, keepdims=True)
        idx_ref[0, r0:r0 + PEER_TOPK, :] = best_i


def peer_route(xf, wq, keys):
    M, D = xf.shape
    assert M % ROUTE_TM == 0
    nblk = M // ROUTE_TM
    out_blk = pl.BlockSpec((1, PEER_E, ROUTE_TM), lambda i: (i, 0, 0))
    idx, gate = pl.pallas_call(
        _route_kernel,
        grid=(nblk,),
        in_specs=[pl.BlockSpec((ROUTE_TM, D), lambda i: (i, 0)),
                  pl.BlockSpec(wq.shape, lambda i: (0, 0)),
                  pl.BlockSpec(keys.shape, lambda i: (0, 0, 0, 0))],
        out_specs=[out_blk, out_blk],
        out_shape=[jax.ShapeDtypeStruct((nblk, PEER_E, ROUTE_TM), jnp.int32),
                   jax.ShapeDtypeStruct((nblk, PEER_E, ROUTE_TM), jnp.float32)],
        compiler_params=pltpu.CompilerParams(dimension_semantics=("arbitrary",),
                                             vmem_limit_bytes=VMEM_LIMIT_BYTES),
        name="peer_route",
    )(xf, wq.astype(jnp.bfloat16), keys.astype(jnp.bfloat16))
    return idx.transpose(0, 2, 1).reshape(M, PEER_E), gate


def mm3(x, w):
    B, T, K = x.shape
    return pmm(x.reshape(B * T, K), w).reshape(B, T, w.shape[1])


def rmsnorm(x, g, eps=1e-6):
    xf = x.astype(jnp.float32)
    y = xf * lax.rsqrt(jnp.mean(xf * xf, -1, keepdims=True) + eps)
    return (y * g.astype(jnp.float32)).astype(x.dtype)


def modulate(x, g, shift, scale):
    return rmsnorm(x, g) * (1 + scale) + shift


def split_cols(p, sizes):
    return jnp.split(p, [int(i) for i in np.cumsum(sizes)], axis=-1)


def flip(t):
    return jnp.flip(t, axis=1)


def grid_to_cols(x):
    B, T, C = x.shape
    rows = T // GRID_W
    return x.reshape(B, rows, GRID_W, C).transpose(0, 2, 1, 3).reshape(B, T, C)


def cols_to_grid(x):
    B, T, C = x.shape
    rows = T // GRID_W
    return x.reshape(B, GRID_W, rows, C).transpose(0, 2, 1, 3).reshape(B, T, C)


def qshift_grid(x):
    B, T, C = x.shape
    rows = T // GRID_W
    g = jnp.pad(x.reshape(B, rows, GRID_W, C), ((0, 0), (1, 1), (1, 1), (0, 0)))
    sel = jnp.arange(C) % 4
    out = jnp.where(sel == 0, g[:, 1:-1, :-2], jnp.where(sel == 1, g[:, 1:-1, 2:],
                    jnp.where(sel == 2, g[:, :-2, 1:-1], g[:, 2:, 1:-1])))
    return out.reshape(B, T, C)


def shift_seq(x):
    prev = jnp.pad(x, ((0, 0), (1, 0), (0, 0)))[:, :-1]
    nxt = jnp.pad(x, ((0, 0), (0, 1), (0, 0)))[:, 1:]
    return jnp.where(jnp.arange(x.shape[-1]) % 2 == 0, prev, nxt)


def dwconv(x, w, b):
    lo, hi = (CONV_W - 1) // 2, CONV_W // 2
    T = x.shape[1]
    xp = jnp.pad(x, ((0, 0), (lo, hi), (0, 0)))
    y = sum(xp[:, j:j + T] * w[j].astype(x.dtype) for j in range(CONV_W))
    return y + b


RW_L = 64
RW_SUB = 16
RW_PAIRS = RW_HEADS // 2


def _bdot(a, b):
    return jnp.dot(a.astype(jnp.bfloat16), b.astype(jnp.bfloat16), preferred_element_type=jnp.float32)


def _bdot_nt(a, b):
    return lax.dot_general(a.astype(jnp.bfloat16), b.astype(jnp.bfloat16), (((1,), (1,)), ((), ())),
                           preferred_element_type=jnp.float32)


def _rwkv_chunk(lws, rs, ks, vs, kks, as_, states, reverse):
    L, L2 = RW_L, 2 * RW_L
    P = range(len(lws))
    row = lax.broadcasted_iota(jnp.int32, (L, L), 0)
    col = lax.broadcasted_iota(jnp.int32, (L, L), 1)
    before = (lambda a, b: a > b) if reverse else (lambda a, b: a < b)
    tri = jnp.logical_not(before(row, col)).astype(jnp.float32)
    row2 = lax.broadcasted_iota(jnp.int32, (L2, L2), 0)
    col2 = lax.broadcasted_iota(jnp.int32, (L2, L2), 1)
    same = (row2 // L) == (col2 // L)
    strict = same & before(col2, row2)
    incl = same & jnp.logical_not(before(row2, col2))
    diag_blk = (row2 // RW_SUB) == (col2 // RW_SUB)
    eye = (row2 == col2).astype(jnp.float32)
    head_a = (lax.broadcasted_iota(jnp.int32, (1, LANES), 1) // RW_HD) == 0

    def stack2(x):
        return jnp.concatenate([jnp.where(head_a, x, 0.0), jnp.where(head_a, 0.0, x)], axis=0)

    cum = [jnp.dot(tri, lws[p], precision=lax.Precision.HIGHEST, preferred_element_type=jnp.float32) for p in P]
    last = 0 if reverse else L - 1
    cum_l = [cum[p][last:last + 1, :] for p in P]
    b = [as_[p] * kks[p] for p in P]
    ginv = [jnp.exp(-cum[p]) for p in P]
    gtail = [jnp.exp(cum_l[p] - cum[p]) for p in P]
    left = [jnp.concatenate([stack2(kks[p] * jnp.exp(cum[p] - lws[p])), stack2(rs[p] * jnp.exp(cum[p]))], axis=0)
            for p in P]
    right = [jnp.concatenate([stack2(b[p] * ginv[p]), stack2(ks[p] * ginv[p])], axis=0) for p in P]
    v2 = [stack2(vs[p]) for p in P]
    g = [_bdot_nt(left[p], right[p]) for p in P]
    wy = [_bdot_nt(left[p], states[p]) for p in P]
    n = [jnp.where(strict, g[p][:L2, :L2], 0.0) for p in P]
    mkk = [jnp.where(strict, g[p][:L2, L2:], 0.0) for p in P]
    mrb = [jnp.where(incl, g[p][L2:, :L2], 0.0) for p in P]
    mrk = [jnp.where(incl, g[p][L2:, L2:], 0.0) for p in P]
    d = [jnp.where(diag_blk, n[p], 0.0) for p in P]
    d2 = [_bdot(d[p], d[p]) for p in P]
    d4 = [_bdot(d2[p], d2[p]) for p in P]
    d8 = [_bdot(d4[p], d4[p]) for p in P]
    tinv = [_bdot(eye - d[p], eye + d2[p]) for p in P]
    tinv = [_bdot(tinv[p], eye + d4[p]) for p in P]
    tinv = [_bdot(tinv[p], eye + d8[p]) for p in P]
    rhs = [wy[p][:L2] + _bdot(mkk[p], v2[p]) for p in P]

    nblk = L // RW_SUB
    zero_blk = jnp.zeros((RW_SUB, LANES), jnp.float32)

    def pick(x, blk):
        lo = blk * RW_SUB
        return jnp.concatenate([x[lo:lo + RW_SUB], x[L + lo:L + lo + RW_SUB]], axis=0)

    def place(blocks):
        half = lambda h: [zero_blk if blocks[j] is None else blocks[j][h * RW_SUB:(h + 1) * RW_SUB]
                          for j in range(nblk)]
        return jnp.concatenate(half(0) + half(1), axis=0)

    u_blocks = [[None] * nblk for _ in P]
    order = list(reversed(range(nblk))) if reverse else list(range(nblk))
    for blk in order:
        for p in P:
            acc = pick(rhs[p], blk)
            if blk != order[0]:
                acc = acc - _bdot(pick(n[p], blk), place(u_blocks[p]))
            cur = [None] * nblk
            cur[blk] = acc
            u_blocks[p][blk] = _bdot(pick(tinv[p], blk), place(cur))
    u = [place(u_blocks[p]) for p in P]
    y2 = [wy[p][L2:] - _bdot(mrb[p], u[p]) + _bdot(mrk[p], v2[p]) for p in P]
    ys = [y2[p][:L] + y2[p][L:] for p in P]
    upd = [_bdot(v2[p].T, stack2(ks[p] * gtail[p])) - _bdot(u[p].T, stack2(b[p] * gtail[p])) for p in P]
    s_news = [states[p] * jnp.exp(cum_l[p]) + upd[p] for p in P]
    return ys, s_news


def _rwkv_kernel(lw_ref, r_ref, k_ref, v_ref, kk_ref, a_ref, s0_ref, y_ref, sT_ref, s_scr, *, reverse):
    c = pl.program_id(1)

    @pl.when(c == 0)
    def _():
        s_scr[...] = s0_ref[0]

    slabs = [pl.ds(p * LANES, LANES) for p in range(RW_PAIRS)]
    ys, s_news = _rwkv_chunk(*[[ref[0, :, sl] for sl in slabs]
                               for ref in (lw_ref, r_ref, k_ref, v_ref, kk_ref, a_ref)],
                             [s_scr[p] for p in range(RW_PAIRS)], reverse)
    for p in range(RW_PAIRS):
        y_ref[0, :, slabs[p]] = ys[p]
        s_scr[p] = s_news[p]

    @pl.when(c == pl.num_programs(1) - 1)
    def _():
        sT_ref[0] = s_scr[...]


def rwkv7_scan(r, lw, k, v, kk, a, S0, reverse=False):
    B, T, C = r.shape
    assert C == RW_PAIRS * LANES and T % RW_L == 0
    nc = T // RW_L
    seq = pl.BlockSpec((1, RW_L, C), (lambda b, c: (b, nc - 1 - c, 0)) if reverse else (lambda b, c: (b, c, 0)))
    st = pl.BlockSpec((1, RW_PAIRS, LANES, LANES), lambda b, c: (b, 0, 0, 0))
    y, sT = pl.pallas_call(
        functools.partial(_rwkv_kernel, reverse=reverse),
        grid=(B, nc),
        in_specs=[seq] * 6 + [st],
        out_specs=[seq, st],
        out_shape=[jax.ShapeDtypeStruct((B, T, C), jnp.float32),
                   jax.ShapeDtypeStruct((B, RW_PAIRS, LANES, LANES), jnp.float32)],
        scratch_shapes=[pltpu.VMEM((RW_PAIRS, LANES, LANES), jnp.float32)],
        compiler_params=pltpu.CompilerParams(dimension_semantics=("arbitrary", "arbitrary"),
                                             vmem_limit_bytes=VMEM_LIMIT_BYTES),
        name="rwkv7_scan",
    )(lw, r, k, v, kk, a, S0)
    return y, sT


def rwkv_stream(p, S0, w0, w_up, a0, a_up, g_up, k_k, k_a, r_k, ln_g, ln_b):
    B, T, _ = p.shape
    r, k, v, wd, ad, gd = split_cols(p, [RW_WIDTH, RW_WIDTH, RW_WIDTH, RW_DECAY_RANK, RW_A_RANK])
    heads = lambda t: t.reshape(B, T, RW_HEADS, RW_HD)
    kk = heads(k * k_k).astype(jnp.float32)
    kk = (kk * lax.rsqrt(jnp.maximum(jnp.sum(kk * kk, -1, keepdims=True), 1e-12))).reshape(B, T, RW_WIDTH)
    g = jax.nn.sigmoid(gd) @ g_up
    per_dir, bonus = [], []
    for d in range(2):
        wl = (w0[d] + jnp.tanh(wd) @ w_up[d]).astype(jnp.float32)
        lw = -jnp.exp(-jax.nn.softplus(-wl) - 0.5)
        a = jax.nn.sigmoid(a0[d] + ad @ a_up[d])
        kd = k * (1 + (a - 1) * k_a)
        per_dir.append(rwkv7_scan(r, lw, kd, v, kk, a, S0[d], reverse=(d == 1)))
        bonus.append(jnp.sum(heads(r) * heads(kd) * r_k, -1, keepdims=True) * heads(v))
    sT = (per_dir[0][1], per_dir[1][1])
    y = (per_dir[0][0] + per_dir[1][0]).astype(jnp.float32).reshape(B, T, RW_HEADS, RW_HD)
    mu = jnp.mean(y, -1, keepdims=True)
    var = jnp.mean(jnp.square(y - mu), -1, keepdims=True)
    yn = ((y - mu) * lax.rsqrt(var + RW_GN_EPS)).reshape(B, T, RW_WIDTH) * ln_g + ln_b
    out = (yn + (bonus[0] + bonus[1]).reshape(B, T, RW_WIDTH)) * g
    return out.astype(p.dtype), sT


GLA_SUB = 16
GLA_PAIRS = GLA_HEADS // 2
assert GLA_DV == LANES and 2 * GLA_DK == LANES


def _gla_kernel(q_ref, k_ref, g_ref, v_ref, s0_ref, o_ref, sT_ref, s_scr, *, reverse):
    c = pl.program_id(1)
    L = GLA_CHUNK

    @pl.when(c == 0)
    def _():
        s_scr[...] = s0_ref[0]

    row = lax.broadcasted_iota(jnp.int32, (L, L), 0)
    col = lax.broadcasted_iota(jnp.int32, (L, L), 1)
    incl = (col >= row) if reverse else (col <= row)
    last = 0 if reverse else L - 1
    lane = lax.broadcasted_iota(jnp.int32, (1, LANES), 1)
    trow = lax.broadcasted_iota(jnp.int32, (L, LANES), 0)
    outs = []
    for p in range(GLA_PAIRS):
        sl = pl.ds(p * LANES, LANES)
        q, k, g = q_ref[0, :, sl], k_ref[0, :, sl], g_ref[0, :, sl]
        b = jnp.dot(incl.astype(jnp.float32), g, precision=lax.Precision.HIGHEST,
                    preferred_element_type=jnp.float32)
        b_l = b[last:last + 1, :]
        qe = q * jnp.exp(b)
        kd = k * jnp.exp(b_l - b)
        qfs, kfs = [], []
        for blk in range(L // GLA_SUB):
            lo, hi = blk * GLA_SUB, (blk + 1) * GLA_SUB
            first = hi - 1 if reverse else lo
            b_ref = b[first:first + 1, :]
            qfs.append(q[lo:hi, :] * jnp.exp(b[lo:hi, :] - b_ref))
            used = (trow >= lo) if reverse else (trow < hi)
            kfs.append(jnp.where(used, k * jnp.exp(jnp.where(used, b_ref - b, 0.0)), 0.0))
        for hh in range(2):
            h = 2 * p + hh
            m = (lane // GLA_DK) == hh
            v = v_ref[0, :, pl.ds(h * GLA_DV, GLA_DV)]
            s_t = s_scr[h]
            a = jnp.concatenate([_bdot_nt(qf, jnp.where(m, kf, 0.0)) for qf, kf in zip(qfs, kfs)], axis=0)
            o = _bdot(jnp.where(incl, a, 0.0), v) + _bdot_nt(qe, s_t)
            s_new = s_t * jnp.exp(b_l) + _bdot(v.T, jnp.where(m, kd, 0.0))
            outs.append((h, o, s_new))
    for h, o, s_new in outs:
        o_ref[0, :, pl.ds(h * GLA_DV, GLA_DV)] = o
        s_scr[h] = s_new

    @pl.when(c == pl.num_programs(1) - 1)
    def _():
        sT_ref[0] = s_scr[...]


def gla_chunked(q, k, g, v, S0, reverse=False):
    B, T, _ = q.shape
    assert T % GLA_CHUNK == 0
    nc = T // GLA_CHUNK
    chunk = (lambda b, c: (b, nc - 1 - c, 0)) if reverse else (lambda b, c: (b, c, 0))
    qk = pl.BlockSpec((1, GLA_CHUNK, GLA_HEADS * GLA_DK), chunk)
    vv = pl.BlockSpec((1, GLA_CHUNK, GLA_HEADS * GLA_DV), chunk)
    st = pl.BlockSpec((1, GLA_HEADS, LANES, LANES), lambda b, c: (b, 0, 0, 0))
    return pl.pallas_call(
        functools.partial(_gla_kernel, reverse=reverse),
        grid=(B, nc),
        in_specs=[qk, qk, qk, vv, st],
        out_specs=[vv, st],
        out_shape=[jax.ShapeDtypeStruct((B, T, GLA_HEADS * GLA_DV), jnp.float32),
                   jax.ShapeDtypeStruct((B, GLA_HEADS, LANES, LANES), jnp.float32)],
        scratch_shapes=[pltpu.VMEM((GLA_HEADS, LANES, LANES), jnp.float32)],
        compiler_params=pltpu.CompilerParams(dimension_semantics=("arbitrary", "arbitrary"),
                                             vmem_limit_bytes=VMEM_LIMIT_BYTES),
        name="gla_chunked",
    )(q, k, g, v, S0)


def gla_stream(p, S0, gate_up, gate_b, norm_g):
    B, T, _ = p.shape
    q, k, v, gd, og = split_cols(p, [GLA_HEADS * GLA_DK, GLA_HEADS * GLA_DK, GLA_HEADS * GLA_DV, GLA_GATE_RANK])
    q = q * GLA_DK ** -0.5
    outs, states = [], []
    for d in range(2):
        glog = jax.nn.log_sigmoid((gd @ gate_up[d] + gate_b[d]).astype(jnp.float32)) / GLA_GATE_NORM
        od, sd = gla_chunked(q, k, glog, v, S0[d], reverse=(d == 1))
        outs.append(od)
        states.append(sd)
    o = (outs[0] + outs[1]).reshape(B, T, GLA_HEADS, GLA_DV)
    o = rmsnorm(o, norm_g.reshape(GLA_HEADS, GLA_DV)).reshape(B, T, GLA_HEADS * GLA_DV)
    return (o * jax.nn.silu(og)).astype(p.dtype), tuple(states)


def ssd_chunked(x, dt, A, Bm, Cm, S0):
    Bsz, T, H, P = x.shape
    N = Bm.shape[-1]
    L = SSD_CHUNK
    n = T // L
    f = lambda t: t.astype(jnp.float32)
    xc = (f(x) * f(dt)[..., None]).reshape(Bsz, n, L, H, P)
    Bc = f(Bm).reshape(Bsz, n, L, H, N)
    Cc = f(Cm).reshape(Bsz, n, L, H, N)
    acs = jnp.cumsum((f(dt) * A).reshape(Bsz, n, L, H).transpose(0, 3, 1, 2), -1)
    mask = jnp.tril(jnp.ones((L, L), bool))
    Lmat = jnp.exp(jnp.where(mask, acs[..., :, None] - acs[..., None, :], -jnp.inf))
    G = jnp.einsum("bclhn,bcshn->bhcls", Cc, Bc) * Lmat
    y_diag = jnp.einsum("bhcls,bcshp->bclhp", G, xc)
    contrib = jnp.einsum("bclhn,bhcl,bclhp->bchpn", Bc, jnp.exp(acs[..., -1:] - acs), xc)
    chunk_decay = jnp.exp(acs[..., -1])

    def step(S, inp):
        dec, st = inp
        return dec[..., None, None] * S + st, S

    S, S_in = lax.scan(step, S0, (jnp.moveaxis(chunk_decay, 2, 0), jnp.moveaxis(contrib, 1, 0)))
    y_off = jnp.einsum("bclhn,bchpn,bhcl->bclhp", Cc, jnp.moveaxis(S_in, 0, 1), jnp.exp(acs))
    return (y_diag + y_off).reshape(Bsz, T, H, P).astype(x.dtype), S


def ssd_stream(p, S0f, S0b, conv_w, conv_b, dt_bias, A_log, D_skip, norm_g):
    B, T, _ = p.shape
    z, xbc, dtr = split_cols(p, [SSD_INNER, SSD_CONV_CH])
    xbc = jax.nn.silu(dwconv(xbc, conv_w, conv_b))
    xs, Bm, Cm = split_cols(xbc, [SSD_INNER, SSD_GROUPS * SSD_N])
    xs = xs.reshape(B, T, SSD_HEADS, SSD_P)
    rep = SSD_HEADS // SSD_GROUPS
    Bm = jnp.repeat(Bm.reshape(B, T, SSD_GROUPS, SSD_N), rep, axis=2)
    Cm = jnp.repeat(Cm.reshape(B, T, SSD_GROUPS, SSD_N), rep, axis=2)
    ys, states = [], []
    for d, S0 in enumerate((S0f, S0b)):
        dt = jax.nn.softplus((dtr + dt_bias[d]).astype(jnp.float32))
        A = -jnp.exp(A_log[d].astype(jnp.float32))
        args = [xs, dt, Bm, Cm]
        if d == 1:
            args = [flip(t) for t in args]
        yd, Sd = ssd_chunked(args[0], args[1], A, args[2], args[3], S0)
        ys.append(flip(yd) if d == 1 else yd)
        states.append(Sd)
    y = (ys[0] + ys[1] + xs * D_skip[:, None]).reshape(B, T, SSD_INNER) * jax.nn.silu(z)
    return rmsnorm(y, norm_g).astype(p.dtype), states[0], states[1]


def linear_scan(a, b, h0):
    b = b.at[:, 0].add(a[:, 0] * h0)
    comb = lambda l, r: (l[0] * r[0], r[0] * l[1] + r[1])
    _, h = lax.associative_scan(comb, (a, b), axis=1)
    return h, h[:, -1]


def lru_stream(p, h0f, h0b, conv_w, conv_b, wa, ba, wx, bx, lam):
    B, T, _ = p.shape
    xr, gate = split_cols(p, [LRU_WIDTH])
    xf = dwconv(xr, conv_w, conv_b).astype(jnp.float32)
    xh = xf.reshape(B, T, LRU_BLOCKS, LRU_BD)
    hs, finals = [], []
    for d, h0 in enumerate((h0f, h0b)):
        r = jax.nn.sigmoid(jnp.einsum("btgi,gij->btgj", xh, wa[d]).reshape(B, T, LRU_WIDTH) + ba[d])
        i = jax.nn.sigmoid(jnp.einsum("btgi,gij->btgj", xh, wx[d]).reshape(B, T, LRU_WIDTH) + bx[d])
        log_a = -LRU_C * r * jax.nn.softplus(-lam[d])
        a = jnp.exp(log_a)
        b = jnp.sqrt(-jnp.expm1(2 * log_a)) * (i * xf)
        if d == 1:
            a, b = flip(a), flip(b)
        h, hT = linear_scan(a, b, h0)
        hs.append(flip(h) if d == 1 else h)
        finals.append(hT)
    y = (hs[0] + hs[1]).astype(p.dtype) * jax.nn.gelu(gate)
    return y, finals[0], finals[1]


def even_mixer(u, uc, w_in, mu, rw_p, gla_p):
    B = u.shape[0]
    p, pc = mm3(u, w_in), mm3(uc, w_in)
    rw, gl = p[..., :RW_COLS], p[..., RW_COLS:]
    rwc, glc = pc[..., :RW_COLS], pc[..., RW_COLS:]
    rw = rw + mu * (qshift_grid(rw) - rw)
    rwc = rwc + mu * (shift_seq(rwc) - rwc)
    z_rw = jnp.zeros((B, RW_PAIRS, LANES, LANES), jnp.float32)
    yc_rw, s_ctx = rwkv_stream(rwc, (z_rw, z_rw), *rw_p)
    y_rw, _ = rwkv_stream(rw, s_ctx, *rw_p)
    z_gla = jnp.zeros((B, GLA_HEADS, LANES, LANES), jnp.float32)
    yc_gl, g_ctx = gla_stream(glc, (z_gla, z_gla), *gla_p)
    y_gl, _ = gla_stream(gl, g_ctx, *gla_p)
    return jnp.concatenate([y_rw, y_gl], -1), jnp.concatenate([yc_rw, yc_gl], -1)


def odd_mixer(u, uc, w_in, ssd_p, lru_p):
    B = u.shape[0]
    p = grid_to_cols(mm3(u, w_in))
    pc = mm3(uc, w_in)
    z_s = jnp.zeros((B, SSD_HEADS, SSD_P, SSD_N), jnp.float32)
    yc_s, Sf, Sb = ssd_stream(pc[..., :SSD_COLS], z_s, z_s, *ssd_p)
    y_s, _, _ = ssd_stream(p[..., :SSD_COLS], Sf, Sb, *ssd_p)
    z_l = jnp.zeros((B, LRU_WIDTH), jnp.float32)
    yc_l, hf, hb = lru_stream(pc[..., SSD_COLS:], z_l, z_l, *lru_p)
    y_l, _, _ = lru_stream(p[..., SSD_COLS:], hf, hb, *lru_p)
    return cols_to_grid(jnp.concatenate([y_s, y_l], -1)), jnp.concatenate([yc_s, yc_l], -1)


def peer(x, wq, keys, u_tab, v_tab):
    B, T, D = x.shape
    M = B * T
    xf = x.reshape(M, D)
    idx, gate_t = peer_route(xf, wq, keys)
    return peer_experts(xf, idx, gate_t, u_tab, v_tab).reshape(B, T, D)


def kernel(x, c, ctx, c_ctx, ada_w, ada_b, norm1_g, norm2_g, ev_w_in, rw_mu, rw_w0, rw_w_up, rw_a0, rw_a_up,
           rw_g_up, rw_k_k, rw_k_a, rw_r_k, rw_ln_g, rw_ln_b, gla_gate_up, gla_gate_b, gla_norm_g, od_w_in,
           ssd_conv_w, ssd_conv_b, ssd_dt_bias, ssd_A_log, ssd_D, ssd_norm_g, lru_conv_w, lru_conv_b, lru_wa,
           lru_ba, lru_wx, lru_bx, lru_lam, w_out, peer_wq, peer_keys, peer_u, peer_v, final_g):
    h, hc = x, ctx
    sc, scc = jax.nn.silu(c), jax.nn.silu(c_ctx)
    for l in range(DEPTH):
        last = l == DEPTH - 1
        i = l // 2
        mod = jnp.split((sc @ ada_w[l] + ada_b[l])[:, None, :], 6, axis=-1)
        modc = jnp.split(scc @ ada_w[l] + ada_b[l], 6, axis=-1)
        u = modulate(h, norm1_g[l], mod[0], mod[1])
        uc = modulate(hc, norm1_g[l], modc[0], modc[1])
        if l % 2 == 0:
            rw_p = (rw_w0[i], rw_w_up[i], rw_a0[i], rw_a_up[i], rw_g_up[i], rw_k_k[i], rw_k_a[i], rw_r_k[i],
                    rw_ln_g[i], rw_ln_b[i])
            gla_p = (gla_gate_up[i], gla_gate_b[i], gla_norm_g[i])
            y, yc = even_mixer(u, uc, ev_w_in[i], rw_mu[i], rw_p, gla_p)
        else:
            ssd_p = (ssd_conv_w[i], ssd_conv_b[i], ssd_dt_bias[i], ssd_A_log[i], ssd_D[i], ssd_norm_g[i])
            lru_p = (lru_conv_w[i], lru_conv_b[i], lru_wa[i], lru_ba[i], lru_wx[i], lru_bx[i], lru_lam[i])
            y, yc = odd_mixer(u, uc, od_w_in[i], ssd_p, lru_p)
        h = h + mod[2] * mm3(y, w_out[l])
        u = modulate(h, norm2_g[l], mod[3], mod[4])
        h = h + mod[5] * peer(u, peer_wq[l], peer_keys[l], peer_u[l], peer_v[l])
        if not last:
            hc = hc + modc[2] * mm3(yc, w_out[l])
            uc = modulate(hc, norm2_g[l], modc[3], modc[4])
            hc = hc + modc[5] * peer(uc, peer_wq[l], peer_keys[l], peer_u[l], peer_v[l])
    return rmsnorm(h, final_g)
```

```python
import functools

import jax
import jax.numpy as jnp
from jax import lax
import numpy as np
from jax.experimental import pallas as pl
from jax.experimental.pallas import tpu as pltpu

D_MODEL = 1024
DEPTH = 4
GRID_W = 64
MIX_WIDTH = D_MODEL

RW_WIDTH = MIX_WIDTH // 2
RW_HD = 64
RW_HEADS = RW_WIDTH // RW_HD
RW_DECAY_RANK = 32
RW_A_RANK = 32
RW_GATE_RANK = 64
RW_GN_EPS = 64e-5
RW_COLS = 3 * RW_WIDTH + RW_DECAY_RANK + RW_A_RANK + RW_GATE_RANK
GLA_HEADS = 4
GLA_DV = (MIX_WIDTH - RW_WIDTH) // GLA_HEADS
GLA_DK = GLA_DV // 2
GLA_GATE_RANK = 16
GLA_GATE_NORM = 16.0
GLA_CHUNK = 64
SSD_INNER = MIX_WIDTH // 2
SSD_P = 64
SSD_HEADS = SSD_INNER // SSD_P
SSD_N = 128
SSD_GROUPS = 2
SSD_CHUNK = 128
SSD_CONV_CH = SSD_INNER + 2 * SSD_GROUPS * SSD_N
SSD_COLS = SSD_INNER + SSD_CONV_CH + SSD_HEADS
LRU_WIDTH = MIX_WIDTH - SSD_INNER
LRU_BLOCKS = 8
LRU_BD = LRU_WIDTH // LRU_BLOCKS
LRU_C = 8.0
CONV_W = 4
PEER_HEADS = 8
N_KEYS = 128
PEER_TOPK = 16
PEER_DQ = 256
PEER_BLOCK = 128

VMEM_LIMIT_BYTES = 56 * 1024 * 1024


def _mm_kernel(a_ref, b_ref, o_ref):
    o_ref[...] = jnp.dot(a_ref[...].astype(jnp.bfloat16), b_ref[...], preferred_element_type=jnp.float32)


def pmm(a, b, tm=256):
    M, K = a.shape
    N = b.shape[1]
    assert M % tm == 0
    return pl.pallas_call(
        _mm_kernel,
        grid=(M // tm,),
        in_specs=[pl.BlockSpec((tm, K), lambda i: (i, 0)), pl.BlockSpec((K, N), lambda i: (0, 0))],
        out_specs=pl.BlockSpec((tm, N), lambda i: (i, 0)),
        out_shape=jax.ShapeDtypeStruct((M, N), jnp.float32),
        compiler_params=pltpu.CompilerParams(dimension_semantics=("arbitrary",),
                                             vmem_limit_bytes=VMEM_LIMIT_BYTES),
        name="pmm",
    )(a, b.astype(jnp.bfloat16))


SUBLANES = 8
LANES = 128
PEER_E = PEER_HEADS * PEER_TOPK
PEER_TB = 8
PEER_SLOTS = 3


def _fold8(p):
    sub = lax.broadcasted_iota(jnp.int32, (SUBLANES, LANES), 0)

    def comb(a, b, k):
        fa = a + pltpu.roll(a, SUBLANES - k, 0)
        fb = b + pltpu.roll(b, k, 0)
        return jnp.where((sub & k) == 0, fa, fb)

    return comb(comb(comb(p[0], p[4], 4), comb(p[2], p[6], 4), 2),
                comb(comb(p[1], p[5], 4), comb(p[3], p[7], 4), 2), 1)


def _peer_kernel(idx_cur_ref, idx_nx1_ref, idx_nxt_ref, gate_ref, x_ref, tab_ref, o_ref, buf, sem):
    i = pl.program_id(0)
    n = pl.num_programs(0)
    slot = i % PEER_SLOTS
    nslot = (i + PEER_SLOTS - 1) % PEER_SLOTS

    def issue(idx_ref, t, s, e0=0, e1=PEER_E):
        for e in range(e0, e1):
            j = t * PEER_E + e
            pltpu.make_async_copy(tab_ref.at[idx_ref[0, 0, j]], buf.at[s, j], sem.at[s]).start(priority=e % 2)

    def wait_slot(s):
        pltpu.make_async_copy(buf.at[s], buf.at[s], sem.at[s]).wait()

    @pl.when(i == 0)
    def _():
        lax.fori_loop(0, PEER_TB, lambda t, c: (issue(idx_cur_ref, t, 0), c)[1], 0)
        lax.fori_loop(0, PEER_TB, lambda t, c: (issue(idx_nx1_ref, t, 1), c)[1], 0)

    wait_slot(slot)

    n_groups = PEER_E // SUBLANES
    per_group = PEER_E // (2 * n_groups)
    n_acc = 4

    def token(t, carry):
        xt = x_ref[t]
        base = t * PEER_E
        tok = (i % (ROUTE_TM // PEER_TB)) * PEER_TB + t
        onehot = (lax.broadcasted_iota(jnp.int32, (SUBLANES, ROUTE_TM), 1) == tok).astype(jnp.float32)
        coefs = []
        for g in range(n_groups):
            rows = [buf[slot, base + g * SUBLANES + j, 0] for j in range(SUBLANES)]
            issue(idx_nxt_ref, t, nslot, g * per_group, (g + 1) * per_group)
            a = jnp.sum(_fold8([r * xt for r in rows]), axis=1, keepdims=True)
            gate = jnp.sum(gate_ref[0, pl.ds(g * SUBLANES, SUBLANES), :] * onehot, axis=1, keepdims=True)
            coefs.append(jnp.broadcast_to(jax.nn.gelu(a) * gate, (SUBLANES, LANES)))
        accs = [jnp.zeros((SUBLANES, LANES), jnp.float32) for _ in range(n_acc)]
        for g in range(n_groups):
            rows = [buf[slot, base + g * SUBLANES + j, 1] for j in range(SUBLANES)]
            issue(idx_nxt_ref, t, nslot, (n_groups + g) * per_group, (n_groups + g + 1) * per_group)
            for j in range(SUBLANES):
                accs[j % n_acc] = accs[j % n_acc] + coefs[g][j:j + 1, :] * rows[j]
        o_ref[t] = (accs[0] + accs[1]) + (accs[2] + accs[3])
        return carry

    lax.fori_loop(0, PEER_TB, token, 0)

    @pl.when(i == n - 1)
    def _():
        wait_slot((i + 1) % PEER_SLOTS)
        wait_slot(nslot)


def peer_experts(xf, idx, gate_t, u_tab, v_tab):
    M, D = xf.shape
    NE = u_tab.shape[0]
    assert D == SUBLANES * LANES and M % ROUTE_TM == 0 and ROUTE_TM % PEER_TB == 0 and idx.shape == (M, PEER_E)
    assert gate_t.shape == (M // ROUTE_TM, PEER_E, ROUTE_TM)
    nblk = M // PEER_TB
    tab = jnp.stack([u_tab.reshape(NE, SUBLANES, LANES), v_tab.reshape(NE, SUBLANES, LANES)], axis=1)
    idx3 = idx.astype(jnp.int32).reshape(nblk, 1, PEER_TB * PEER_E)
    x3 = xf.reshape(M, SUBLANES, LANES)
    smem_blk = (1, 1, PEER_TB * PEER_E)
    out = pl.pallas_call(
        _peer_kernel,
        grid=(nblk,),
        in_specs=[
            pl.BlockSpec(smem_blk, lambda i: (i, 0, 0), memory_space=pltpu.SMEM),
            pl.BlockSpec(smem_blk, lambda i: (jnp.minimum(i + 1, nblk - 1), 0, 0), memory_space=pltpu.SMEM),
            pl.BlockSpec(smem_blk, lambda i: (jnp.minimum(i + 2, nblk - 1), 0, 0), memory_space=pltpu.SMEM),
            pl.BlockSpec((1, PEER_E, ROUTE_TM), lambda i: (i // (ROUTE_TM // PEER_TB), 0, 0)),
            pl.BlockSpec((PEER_TB, SUBLANES, LANES), lambda i: (i, 0, 0)),
            pl.BlockSpec(memory_space=pl.ANY),
        ],
        out_specs=pl.BlockSpec((PEER_TB, SUBLANES, LANES), lambda i: (i, 0, 0)),
        out_shape=jax.ShapeDtypeStruct((M, SUBLANES, LANES), jnp.float32),
        scratch_shapes=[
            pltpu.VMEM((PEER_SLOTS, PEER_TB * PEER_E, 2, SUBLANES, LANES), jnp.float32),
            pltpu.SemaphoreType.DMA((PEER_SLOTS,)),
        ],
        compiler_params=pltpu.CompilerParams(dimension_semantics=("arbitrary",),
                                             vmem_limit_bytes=VMEM_LIMIT_BYTES),
        name="peer_experts",
    )(idx3, idx3, idx3, gate_t.astype(jnp.float32), x3, tab)
    return out.reshape(M, D)


ROUTE_TM = 128
PEER_HALF = PEER_DQ // 2


def _topk_rows(s, k, rank=None, payload=None):
    if rank is None:
        rank = lax.broadcasted_iota(jnp.int32, s.shape, 0)
    big = jnp.iinfo(jnp.int32).max
    vals, sel = [], []
    for _ in range(k):
        m = jnp.max(s, axis=0, keepdims=True)
        pos = jnp.min(jnp.where(s == m, rank, big), axis=0, keepdims=True)
        hit = rank == pos
        vals.append(m)
        sel.append(pos if payload is None else jnp.max(jnp.where(hit, payload, -1), axis=0, keepdims=True))
        s = jnp.where(hit, -jnp.inf, s)
    return jnp.concatenate(vals, axis=0), jnp.concatenate(sel, axis=0)


def _product_candidates(top_s, top_i):
    K = PEER_TOPK
    s1, s2 = top_s
    i1, i2 = top_i
    TM = s1.shape[1]
    vals, flat, eid = [], [], []
    for i in range(2):
        nj = K // (i + 1)
        j = lax.broadcasted_iota(jnp.int32, (nj, TM), 0)
        vals.append(s1[i:i + 1, :] + s2[:nj, :])
        flat.append(i * K + j)
        eid.append(i1[i:i + 1, :] * N_KEYS + i2[:nj, :])
    for j in range(K // 3):
        ni_end = K // (j + 1)
        rows = K if ni_end > SUBLANES else SUBLANES
        i = lax.broadcasted_iota(jnp.int32, (rows, TM), 0)
        ok = (i >= 2) & (i < ni_end)
        vals.append(jnp.where(ok, s1[:rows, :] + s2[j:j + 1, :], -jnp.inf))
        flat.append(i * K + j)
        eid.append(i1[:rows, :] * N_KEYS + i2[j:j + 1, :])
    cat = lambda xs: jnp.concatenate(xs, axis=0)
    return cat(vals), cat(flat), cat(eid)


def _route_kernel(x_ref, wq_ref, keys_ref, idx_ref, gate_ref):
    q = jnp.dot(x_ref[...].astype(jnp.bfloat16), wq_ref[...], preferred_element_type=jnp.float32)
    for h in range(PEER_HEADS):
        top_s, top_i = [], []
        for half in range(2):
            c0 = (h * 2 + half) * PEER_HALF
            qh = q[:, c0:c0 + PEER_HALF].astype(jnp.bfloat16)
            s = lax.dot_general(keys_ref[h, half], qh, (((1,), (1,)), ((), ())),
                                preferred_element_type=jnp.float32)
            ts, ti = _topk_rows(s, PEER_TOPK)
            top_s.append(ts)
            top_i.append(ti)
        cand_s, cand_flat, cand_i = _product_candidates(top_s, top_i)
        best_s, best_i = _topk_rows(cand_s, PEER_TOPK, cand_flat, cand_i)
        e = jnp.exp(best_s - best_s[0:1, :])
        r0 = h * PEER_TOPK
        gate_ref[0, r0:r0 + PEER_TOPK, :] = e / jnp.sum(e, axis=0, keepdims=True)
        idx_ref[0, r0:r0 + PEER_TOPK, :] = best_i


def peer_route(xf, wq, keys):
    M, D = xf.shape
    assert M % ROUTE_TM == 0
    nblk = M // ROUTE_TM
    out_blk = pl.BlockSpec((1, PEER_E, ROUTE_TM), lambda i: (i, 0, 0))
    idx, gate = pl.pallas_call(
        _route_kernel,
        grid=(nblk,),
        in_specs=[pl.BlockSpec((ROUTE_TM, D), lambda i: (i, 0)),
                  pl.BlockSpec(wq.shape, lambda i: (0, 0)),
                  pl.BlockSpec(keys.shape, lambda i: (0, 0, 0, 0))],
        out_specs=[out_blk, out_blk],
        out_shape=[jax.ShapeDtypeStruct((nblk, PEER_E, ROUTE_TM), jnp.int32),
                   jax.ShapeDtypeStruct((nblk, PEER_E, ROUTE_TM), jnp.float32)],
        compiler_params=pltpu.CompilerParams(dimension_semantics=("arbitrary",),
                                             vmem_limit_bytes=VMEM_LIMIT_BYTES),
        name="peer_route",
    )(xf, wq.astype(jnp.bfloat16), keys.astype(jnp.bfloat16))
    return idx.transpose(0, 2, 1).reshape(M, PEER_E), gate


def mm3(x, w):
    B, T, K = x.shape
    return pmm(x.reshape(B * T, K), w).reshape(B, T, w.shape[1])


def rmsnorm(x, g, eps=1e-6):
    xf = x.astype(jnp.float32)
    y = xf * lax.rsqrt(jnp.mean(xf * xf, -1, keepdims=True) + eps)
    return (y * g.astype(jnp.float32)).astype(x.dtype)


def modulate(x, g, shift, scale):
    return rmsnorm(x, g) * (1 + scale) + shift


def split_cols(p, sizes):
    return jnp.split(p, [int(i) for i in np.cumsum(sizes)], axis=-1)


def flip(t):
    return jnp.flip(t, axis=1)


def grid_to_cols(x):
    B, T, C = x.shape
    rows = T // GRID_W
    return x.reshape(B, rows, GRID_W, C).transpose(0, 2, 1, 3).reshape(B, T, C)


def cols_to_grid(x):
    B, T, C = x.shape
    rows = T // GRID_W
    return x.reshape(B, GRID_W, rows, C).transpose(0, 2, 1, 3).reshape(B, T, C)


def qshift_grid(x):
    B, T, C = x.shape
    rows = T // GRID_W
    g = jnp.pad(x.reshape(B, rows, GRID_W, C), ((0, 0), (1, 1), (1, 1), (0, 0)))
    sel = jnp.arange(C) % 4
    out = jnp.where(sel == 0, g[:, 1:-1, :-2], jnp.where(sel == 1, g[:, 1:-1, 2:],
                    jnp.where(sel == 2, g[:, :-2, 1:-1], g[:, 2:, 1:-1])))
    return out.reshape(B, T, C)


def shift_seq(x):
    prev = jnp.pad(x, ((0, 0), (1, 0), (0, 0)))[:, :-1]
    nxt = jnp.pad(x, ((0, 0), (0, 1), (0, 0)))[:, 1:]
    return jnp.where(jnp.arange(x.shape[-1]) % 2 == 0, prev, nxt)


def dwconv(x, w, b):
    y = lax.conv_general_dilated(x, w[:, None, :].astype(x.dtype), (1,), [((CONV_W - 1) // 2, CONV_W // 2)],
                                 dimension_numbers=("NWC", "WIO", "NWC"), feature_group_count=x.shape[-1])
    return y + b


RW_L = 64
RW_SUB = 16
RW_PAIRS = RW_HEADS // 2


def _bdot(a, b):
    return jnp.dot(a.astype(jnp.bfloat16), b.astype(jnp.bfloat16), preferred_element_type=jnp.float32)


def _bdot_nt(a, b):
    return lax.dot_general(a.astype(jnp.bfloat16), b.astype(jnp.bfloat16), (((1,), (1,)), ((), ())),
                           preferred_element_type=jnp.float32)


def _rwkv_chunk(lws, rs, ks, vs, kks, as_, states, reverse):
    L, L2 = RW_L, 2 * RW_L
    P = range(len(lws))
    row = lax.broadcasted_iota(jnp.int32, (L, L), 0)
    col = lax.broadcasted_iota(jnp.int32, (L, L), 1)
    before = (lambda a, b: a > b) if reverse else (lambda a, b: a < b)
    tri = jnp.logical_not(before(row, col)).astype(jnp.float32)
    row2 = lax.broadcasted_iota(jnp.int32, (L2, L2), 0)
    col2 = lax.broadcasted_iota(jnp.int32, (L2, L2), 1)
    same = (row2 // L) == (col2 // L)
    strict = same & before(col2, row2)
    incl = same & jnp.logical_not(before(row2, col2))
    diag_blk = (row2 // RW_SUB) == (col2 // RW_SUB)
    eye = (row2 == col2).astype(jnp.float32)
    head_a = (lax.broadcasted_iota(jnp.int32, (1, LANES), 1) // RW_HD) == 0

    def stack2(x):
        return jnp.concatenate([jnp.where(head_a, x, 0.0), jnp.where(head_a, 0.0, x)], axis=0)

    cum = [jnp.dot(tri, lws[p], precision=lax.Precision.HIGHEST, preferred_element_type=jnp.float32) for p in P]
    last = 0 if reverse else L - 1
    cum_l = [cum[p][last:last + 1, :] for p in P]
    b = [as_[p] * kks[p] for p in P]
    ginv = [jnp.exp(-cum[p]) for p in P]
    gtail = [jnp.exp(cum_l[p] - cum[p]) for p in P]
    left = [jnp.concatenate([stack2(kks[p] * jnp.exp(cum[p] - lws[p])), stack2(rs[p] * jnp.exp(cum[p]))], axis=0)
            for p in P]
    right = [jnp.concatenate([stack2(b[p] * ginv[p]), stack2(ks[p] * ginv[p])], axis=0) for p in P]
    v2 = [stack2(vs[p]) for p in P]
    g = [_bdot_nt(left[p], right[p]) for p in P]
    wy = [_bdot_nt(left[p], states[p]) for p in P]
    n = [jnp.where(strict, g[p][:L2, :L2], 0.0) for p in P]
    mkk = [jnp.where(strict, g[p][:L2, L2:], 0.0) for p in P]
    mrb = [jnp.where(incl, g[p][L2:, :L2], 0.0) for p in P]
    mrk = [jnp.where(incl, g[p][L2:, L2:], 0.0) for p in P]
    d = [jnp.where(diag_blk, n[p], 0.0) for p in P]
    d2 = [_bdot(d[p], d[p]) for p in P]
    d4 = [_bdot(d2[p], d2[p]) for p in P]
    d8 = [_bdot(d4[p], d4[p]) for p in P]
    tinv = [_bdot(eye - d[p], eye + d2[p]) for p in P]
    tinv = [_bdot(tinv[p], eye + d4[p]) for p in P]
    tinv = [_bdot(tinv[p], eye + d8[p]) for p in P]
    rhs = [wy[p][:L2] + _bdot(mkk[p], v2[p]) for p in P]

    nblk = L // RW_SUB
    zero_blk = jnp.zeros((RW_SUB, LANES), jnp.float32)

    def pick(x, blk):
        lo = blk * RW_SUB
        return jnp.concatenate([x[lo:lo + RW_SUB], x[L + lo:L + lo + RW_SUB]], axis=0)

    def place(blocks):
        half = lambda h: [zero_blk if blocks[j] is None else blocks[j][h * RW_SUB:(h + 1) * RW_SUB]
                          for j in range(nblk)]
        return jnp.concatenate(half(0) + half(1), axis=0)

    u_blocks = [[None] * nblk for _ in P]
    order = list(reversed(range(nblk))) if reverse else list(range(nblk))
    for blk in order:
        for p in P:
            acc = pick(rhs[p], blk)
            if blk != order[0]:
                acc = acc - _bdot(pick(n[p], blk), place(u_blocks[p]))
            cur = [None] * nblk
            cur[blk] = acc
            u_blocks[p][blk] = _bdot(pick(tinv[p], blk), place(cur))
    u = [place(u_blocks[p]) for p in P]
    y2 = [wy[p][L2:] - _bdot(mrb[p], u[p]) + _bdot(mrk[p], v2[p]) for p in P]
    ys = [y2[p][:L] + y2[p][L:] for p in P]
    upd = [_bdot(v2[p].T, stack2(ks[p] * gtail[p])) - _bdot(u[p].T, stack2(b[p] * gtail[p])) for p in P]
    s_news = [states[p] * jnp.exp(cum_l[p]) + upd[p] for p in P]
    return ys, s_news


def _rwkv_kernel(lw_ref, r_ref, k_ref, v_ref, kk_ref, a_ref, s0_ref, y_ref, sT_ref, s_scr, *, reverse):
    c = pl.program_id(1)

    @pl.when(c == 0)
    def _():
        s_scr[...] = s0_ref[0]

    slabs = [pl.ds(p * LANES, LANES) for p in range(RW_PAIRS)]
    ys, s_news = _rwkv_chunk(*[[ref[0, :, sl] for sl in slabs]
                               for ref in (lw_ref, r_ref, k_ref, v_ref, kk_ref, a_ref)],
                             [s_scr[p] for p in range(RW_PAIRS)], reverse)
    for p in range(RW_PAIRS):
        y_ref[0, :, slabs[p]] = ys[p]
        s_scr[p] = s_news[p]

    @pl.when(c == pl.num_programs(1) - 1)
    def _():
        sT_ref[0] = s_scr[...]


def rwkv7_scan(r, lw, k, v, kk, a, S0, reverse=False):
    B, T, C = r.shape
    assert C == RW_PAIRS * LANES and T % RW_L == 0
    nc = T // RW_L
    seq = pl.BlockSpec((1, RW_L, C), (lambda b, c: (b, nc - 1 - c, 0)) if reverse else (lambda b, c: (b, c, 0)))
    st = pl.BlockSpec((1, RW_PAIRS, LANES, LANES), lambda b, c: (b, 0, 0, 0))
    y, sT = pl.pallas_call(
        functools.partial(_rwkv_kernel, reverse=reverse),
        grid=(B, nc),
        in_specs=[seq] * 6 + [st],
        out_specs=[seq, st],
        out_shape=[jax.ShapeDtypeStruct((B, T, C), jnp.float32),
                   jax.ShapeDtypeStruct((B, RW_PAIRS, LANES, LANES), jnp.float32)],
        scratch_shapes=[pltpu.VMEM((RW_PAIRS, LANES, LANES), jnp.float32)],
        compiler_params=pltpu.CompilerParams(dimension_semantics=("arbitrary", "arbitrary"),
                                             vmem_limit_bytes=VMEM_LIMIT_BYTES),
        name="rwkv7_scan",
    )(lw, r, k, v, kk, a, S0)
    return y, sT


def rwkv_stream(p, S0, w0, w_up, a0, a_up, g_up, k_k, k_a, r_k, ln_g, ln_b):
    B, T, _ = p.shape
    r, k, v, wd, ad, gd = split_cols(p, [RW_WIDTH, RW_WIDTH, RW_WIDTH, RW_DECAY_RANK, RW_A_RANK])
    heads = lambda t: t.reshape(B, T, RW_HEADS, RW_HD)
    kk = heads(k * k_k).astype(jnp.float32)
    kk = (kk * lax.rsqrt(jnp.maximum(jnp.sum(kk * kk, -1, keepdims=True), 1e-12))).reshape(B, T, RW_WIDTH)
    g = jax.nn.sigmoid(gd) @ g_up
    per_dir, bonus = [], []
    for d in range(2):
        wl = (w0[d] + jnp.tanh(wd) @ w_up[d]).astype(jnp.float32)
        lw = -jnp.exp(-jax.nn.softplus(-wl) - 0.5)
        a = jax.nn.sigmoid(a0[d] + ad @ a_up[d])
        kd = k * (1 + (a - 1) * k_a)
        per_dir.append(rwkv7_scan(r, lw, kd, v, kk, a, S0[d], reverse=(d == 1)))
        bonus.append(jnp.sum(heads(r) * heads(kd) * r_k, -1, keepdims=True) * heads(v))
    sT = (per_dir[0][1], per_dir[1][1])
    y = (per_dir[0][0] + per_dir[1][0]).astype(jnp.float32).reshape(B, T, RW_HEADS, RW_HD)
    mu = jnp.mean(y, -1, keepdims=True)
    var = jnp.mean(jnp.square(y - mu), -1, keepdims=True)
    yn = ((y - mu) * lax.rsqrt(var + RW_GN_EPS)).reshape(B, T, RW_WIDTH) * ln_g + ln_b
    out = (yn + (bonus[0] + bonus[1]).reshape(B, T, RW_WIDTH)) * g
    return out.astype(p.dtype), sT


GLA_SUB = 16
GLA_PAIRS = GLA_HEADS // 2
assert GLA_DV == LANES and 2 * GLA_DK == LANES


def _gla_kernel(q_ref, k_ref, g_ref, v_ref, s0_ref, o_ref, sT_ref, s_scr, *, reverse):
    c = pl.program_id(1)
    L = GLA_CHUNK

    @pl.when(c == 0)
    def _():
        s_scr[...] = s0_ref[0]

    row = lax.broadcasted_iota(jnp.int32, (L, L), 0)
    col = lax.broadcasted_iota(jnp.int32, (L, L), 1)
    incl = (col >= row) if reverse else (col <= row)
    last = 0 if reverse else L - 1
    lane = lax.broadcasted_iota(jnp.int32, (1, LANES), 1)
    trow = lax.broadcasted_iota(jnp.int32, (L, LANES), 0)
    outs = []
    for p in range(GLA_PAIRS):
        sl = pl.ds(p * LANES, LANES)
        q, k, g = q_ref[0, :, sl], k_ref[0, :, sl], g_ref[0, :, sl]
        b = jnp.dot(incl.astype(jnp.float32), g, precision=lax.Precision.HIGHEST,
                    preferred_element_type=jnp.float32)
        b_l = b[last:last + 1, :]
        qe = q * jnp.exp(b)
        kd = k * jnp.exp(b_l - b)
        qfs, kfs = [], []
        for blk in range(L // GLA_SUB):
            lo, hi = blk * GLA_SUB, (blk + 1) * GLA_SUB
            first = hi - 1 if reverse else lo
            b_ref = b[first:first + 1, :]
            qfs.append(q[lo:hi, :] * jnp.exp(b[lo:hi, :] - b_ref))
            used = (trow >= lo) if reverse else (trow < hi)
            kfs.append(jnp.where(used, k * jnp.exp(jnp.where(used, b_ref - b, 0.0)), 0.0))
        for hh in range(2):
            h = 2 * p + hh
            m = (lane // GLA_DK) == hh
            v = v_ref[0, :, pl.ds(h * GLA_DV, GLA_DV)]
            s_t = s_scr[h]
            a = jnp.concatenate([_bdot_nt(qf, jnp.where(m, kf, 0.0)) for qf, kf in zip(qfs, kfs)], axis=0)
            o = _bdot(jnp.where(incl, a, 0.0), v) + _bdot_nt(qe, s_t)
            s_new = s_t * jnp.exp(b_l) + _bdot(v.T, jnp.where(m, kd, 0.0))
            outs.append((h, o, s_new))
    for h, o, s_new in outs:
        o_ref[0, :, pl.ds(h * GLA_DV, GLA_DV)] = o
        s_scr[h] = s_new

    @pl.when(c == pl.num_programs(1) - 1)
    def _():
        sT_ref[0] = s_scr[...]


def gla_chunked(q, k, g, v, S0, reverse=False):
    B, T, _ = q.shape
    assert T % GLA_CHUNK == 0
    nc = T // GLA_CHUNK
    chunk = (lambda b, c: (b, nc - 1 - c, 0)) if reverse else (lambda b, c: (b, c, 0))
    qk = pl.BlockSpec((1, GLA_CHUNK, GLA_HEADS * GLA_DK), chunk)
    vv = pl.BlockSpec((1, GLA_CHUNK, GLA_HEADS * GLA_DV), chunk)
    st = pl.BlockSpec((1, GLA_HEADS, LANES, LANES), lambda b, c: (b, 0, 0, 0))
    return pl.pallas_call(
        functools.partial(_gla_kernel, reverse=reverse),
        grid=(B, nc),
        in_specs=[qk, qk, qk, vv, st],
        out_specs=[vv, st],
        out_shape=[jax.ShapeDtypeStruct((B, T, GLA_HEADS * GLA_DV), jnp.float32),
                   jax.ShapeDtypeStruct((B, GLA_HEADS, LANES, LANES), jnp.float32)],
        scratch_shapes=[pltpu.VMEM((GLA_HEADS, LANES, LANES), jnp.float32)],
        compiler_params=pltpu.CompilerParams(dimension_semantics=("arbitrary", "arbitrary"),
                                             vmem_limit_bytes=VMEM_LIMIT_BYTES),
        name="gla_chunked",
    )(q, k, g, v, S0)


def gla_stream(p, S0, gate_up, gate_b, norm_g):
    B, T, _ = p.shape
    q, k, v, gd, og = split_cols(p, [GLA_HEADS * GLA_DK, GLA_HEADS * GLA_DK, GLA_HEADS * GLA_DV, GLA_GATE_RANK])
    q = q * GLA_DK ** -0.5
    outs, states = [], []
    for d in range(2):
        glog = jax.nn.log_sigmoid((gd @ gate_up[d] + gate_b[d]).astype(jnp.float32)) / GLA_GATE_NORM
        od, sd = gla_chunked(q, k, glog, v, S0[d], reverse=(d == 1))
        outs.append(od)
        states.append(sd)
    o = (outs[0] + outs[1]).reshape(B, T, GLA_HEADS, GLA_DV)
    o = rmsnorm(o, norm_g.reshape(GLA_HEADS, GLA_DV)).reshape(B, T, GLA_HEADS * GLA_DV)
    return (o * jax.nn.silu(og)).astype(p.dtype), tuple(states)


def ssd_chunked(x, dt, A, Bm, Cm, S0):
    Bsz, T, H, P = x.shape
    N = Bm.shape[-1]
    L = SSD_CHUNK
    n = T // L
    f = lambda t: t.astype(jnp.float32)
    xc = (f(x) * f(dt)[..., None]).reshape(Bsz, n, L, H, P)
    Bc = f(Bm).reshape(Bsz, n, L, H, N)
    Cc = f(Cm).reshape(Bsz, n, L, H, N)
    acs = jnp.cumsum((f(dt) * A).reshape(Bsz, n, L, H).transpose(0, 3, 1, 2), -1)
    mask = jnp.tril(jnp.ones((L, L), bool))
    Lmat = jnp.exp(jnp.where(mask, acs[..., :, None] - acs[..., None, :], -jnp.inf))
    G = jnp.einsum("bclhn,bcshn->bhcls", Cc, Bc) * Lmat
    y_diag = jnp.einsum("bhcls,bcshp->bclhp", G, xc)
    contrib = jnp.einsum("bclhn,bhcl,bclhp->bchpn", Bc, jnp.exp(acs[..., -1:] - acs), xc)
    chunk_decay = jnp.exp(acs[..., -1])

    def step(S, inp):
        dec, st = inp
        return dec[..., None, None] * S + st, S

    S, S_in = lax.scan(step, S0, (jnp.moveaxis(chunk_decay, 2, 0), jnp.moveaxis(contrib, 1, 0)))
    y_off = jnp.einsum("bclhn,bchpn,bhcl->bclhp", Cc, jnp.moveaxis(S_in, 0, 1), jnp.exp(acs))
    return (y_diag + y_off).reshape(Bsz, T, H, P).astype(x.dtype), S


def ssd_stream(p, S0f, S0b, conv_w, conv_b, dt_bias, A_log, D_skip, norm_g):
    B, T, _ = p.shape
    z, xbc, dtr = split_cols(p, [SSD_INNER, SSD_CONV_CH])
    xbc = jax.nn.silu(dwconv(xbc, conv_w, conv_b))
    xs, Bm, Cm = split_cols(xbc, [SSD_INNER, SSD_GROUPS * SSD_N])
    xs = xs.reshape(B, T, SSD_HEADS, SSD_P)
    rep = SSD_HEADS // SSD_GROUPS
    Bm = jnp.repeat(Bm.reshape(B, T, SSD_GROUPS, SSD_N), rep, axis=2)
    Cm = jnp.repeat(Cm.reshape(B, T, SSD_GROUPS, SSD_N), rep, axis=2)
    ys, states = [], []
    for d, S0 in enumerate((S0f, S0b)):
        dt = jax.nn.softplus((dtr + dt_bias[d]).astype(jnp.float32))
        A = -jnp.exp(A_log[d].astype(jnp.float32))
        args = [xs, dt, Bm, Cm]
        if d == 1:
            args = [flip(t) for t in args]
        yd, Sd = ssd_chunked(args[0], args[1], A, args[2], args[3], S0)
        ys.append(flip(yd) if d == 1 else yd)
        states.append(Sd)
    y = (ys[0] + ys[1] + xs * D_skip[:, None]).reshape(B, T, SSD_INNER) * jax.nn.silu(z)
    return rmsnorm(y, norm_g).astype(p.dtype), states[0], states[1]


def linear_scan(a, b, h0, reverse=False):
    first, last = (-1, 0) if reverse else (0, -1)
    b = b.at[:, first].add(a[:, first] * h0)
    comb = lambda l, r: (l[0] * r[0], r[0] * l[1] + r[1])
    _, h = lax.associative_scan(comb, (a, b), axis=1, reverse=reverse)
    return h, h[:, last]


def lru_stream(p, h0f, h0b, conv_w, conv_b, wa, ba, wx, bx, lam):
    B, T, _ = p.shape
    xr, gate = split_cols(p, [LRU_WIDTH])
    xf = dwconv(xr, conv_w, conv_b).astype(jnp.float32)
    xh = xf.reshape(B, T, LRU_BLOCKS, LRU_BD)
    hs, finals = [], []
    for d, h0 in enumerate((h0f, h0b)):
        r = jax.nn.sigmoid(jnp.einsum("btgi,gij->btgj", xh, wa[d]).reshape(B, T, LRU_WIDTH) + ba[d])
        i = jax.nn.sigmoid(jnp.einsum("btgi,gij->btgj", xh, wx[d]).reshape(B, T, LRU_WIDTH) + bx[d])
        log_a = -LRU_C * r * jax.nn.softplus(-lam[d])
        a = jnp.exp(log_a)
        b = jnp.sqrt(-jnp.expm1(2 * log_a)) * (i * xf)
        h, hT = linear_scan(a, b, h0, reverse=(d == 1))
        hs.append(h)
        finals.append(hT)
    y = (hs[0] + hs[1]).astype(p.dtype) * jax.nn.gelu(gate)
    return y, finals[0], finals[1]


def even_mixer(u, uc, w_in, mu, rw_p, gla_p):
    B = u.shape[0]
    p, pc = mm3(u, w_in), mm3(uc, w_in)
    rw, gl = p[..., :RW_COLS], p[..., RW_COLS:]
    rwc, glc = pc[..., :RW_COLS], pc[..., RW_COLS:]
    rw = rw + mu * (qshift_grid(rw) - rw)
    rwc = rwc + mu * (shift_seq(rwc) - rwc)
    z_rw = jnp.zeros((B, RW_PAIRS, LANES, LANES), jnp.float32)
    yc_rw, s_ctx = rwkv_stream(rwc, (z_rw, z_rw), *rw_p)
    y_rw, _ = rwkv_stream(rw, s_ctx, *rw_p)
    z_gla = jnp.zeros((B, GLA_HEADS, LANES, LANES), jnp.float32)
    yc_gl, g_ctx = gla_stream(glc, (z_gla, z_gla), *gla_p)
    y_gl, _ = gla_stream(gl, g_ctx, *gla_p)
    return jnp.concatenate([y_rw, y_gl], -1), jnp.concatenate([yc_rw, yc_gl], -1)


def odd_mixer(u, uc, w_in, ssd_p, lru_p):
    B = u.shape[0]
    p = grid_to_cols(mm3(u, w_in))
    pc = mm3(uc, w_in)
    z_s = jnp.zeros((B, SSD_HEADS, SSD_P, SSD_N), jnp.float32)
    yc_s, Sf, Sb = ssd_stream(pc[..., :SSD_COLS], z_s, z_s, *ssd_p)
    y_s, _, _ = ssd_stream(p[..., :SSD_COLS], Sf, Sb, *ssd_p)
    z_l = jnp.zeros((B, LRU_WIDTH), jnp.float32)
    yc_l, hf, hb = lru_stream(pc[..., SSD_COLS:], z_l, z_l, *lru_p)
    y_l, _, _ = lru_stream(p[..., SSD_COLS:], hf, hb, *lru_p)
    return cols_to_grid(jnp.concatenate([y_s, y_l], -1)), jnp.concatenate([yc_s, yc_l], -1)


def peer(x, wq, keys, u_tab, v_tab):
    B, T, D = x.shape
    M = B * T
    xf = x.reshape(M, D)
    idx, gate_t = peer_route(xf, wq, keys)
    return peer_experts(xf, idx, gate_t, u_tab, v_tab).reshape(B, T, D)


def kernel(x, c, ctx, c_ctx, ada_w, ada_b, norm1_g, norm2_g, ev_w_in, rw_mu, rw_w0, rw_w_up, rw_a0, rw_a_up,
           rw_g_up, rw_k_k, rw_k_a, rw_r_k, rw_ln_g, rw_ln_b, gla_gate_up, gla_gate_b, gla_norm_g, od_w_in,
           ssd_conv_w, ssd_conv_b, ssd_dt_bias, ssd_A_log, ssd_D, ssd_norm_g, lru_conv_w, lru_conv_b, lru_wa,
           lru_ba, lru_wx, lru_bx, lru_lam, w_out, peer_wq, peer_keys, peer_u, peer_v, final_g):
    h, hc = x, ctx
    sc, scc = jax.nn.silu(c), jax.nn.silu(c_ctx)
    for l in range(DEPTH):
        last = l == DEPTH - 1
        i = l // 2
        mod = jnp.split((sc @ ada_w[l] + ada_b[l])[:, None, :], 6, axis=-1)
        modc = jnp.split(scc @ ada_w[l] + ada_b[l], 6, axis=-1)
        u = modulate(h, norm1_g[l], mod[0], mod[1])
        uc = modulate(hc, norm1_g[l], modc[0], modc[1])
        if l % 2 == 0:
            rw_p = (rw_w0[i], rw_w_up[i], rw_a0[i], rw_a_up[i], rw_g_up[i], rw_k_k[i], rw_k_a[i], rw_r_k[i],
                    rw_ln_g[i], rw_ln_b[i])
            gla_p = (gla_gate_up[i], gla_gate_b[i], gla_norm_g[i])
            y, yc = even_mixer(u, uc, ev_w_in[i], rw_mu[i], rw_p, gla_p)
        else:
            ssd_p = (ssd_conv_w[i], ssd_conv_b[i], ssd_dt_bias[i], ssd_A_log[i], ssd_D[i], ssd_norm_g[i])
            lru_p = (lru_conv_w[i], lru_conv_b[i], lru_wa[i], lru_ba[i], lru_wx[i], lru_bx[i], lru_lam[i])
            y, yc = odd_mixer(u, uc, od_w_in[i], ssd_p, lru_p)
        h = h + mod[2] * mm3(y, w_out[l])
        u = modulate(h, norm2_g[l], mod[3], mod[4])
        h = h + mod[5] * peer(u, peer_wq[l], peer_keys[l], peer_u[l], peer_v[l])
        if not last:
            hc = hc + modc[2] * mm3(yc, w_out[l])
            uc = modulate(hc, norm2_g[l], modc[3], modc[4])
            hc = hc + modc[5] * peer(uc, peer_wq[l], peer_keys[l], peer_u[l], peer_v[l])
    return rmsnorm(h, final_g)
```

```python
import functools

import jax
import jax.numpy as jnp
from jax import lax
import numpy as np
from jax.experimental import pallas as pl
from jax.experimental.pallas import tpu as pltpu

D_MODEL = 1024
DEPTH = 4
GRID_W = 64
MIX_WIDTH = D_MODEL

RW_WIDTH = MIX_WIDTH // 2
RW_HD = 64
RW_HEADS = RW_WIDTH // RW_HD
RW_DECAY_RANK = 32
RW_A_RANK = 32
RW_GATE_RANK = 64
RW_GN_EPS = 64e-5
RW_COLS = 3 * RW_WIDTH + RW_DECAY_RANK + RW_A_RANK + RW_GATE_RANK
GLA_HEADS = 4
GLA_DV = (MIX_WIDTH - RW_WIDTH) // GLA_HEADS
GLA_DK = GLA_DV // 2
GLA_GATE_RANK = 16
GLA_GATE_NORM = 16.0
GLA_CHUNK = 64
SSD_INNER = MIX_WIDTH // 2
SSD_P = 64
SSD_HEADS = SSD_INNER // SSD_P
SSD_N = 128
SSD_GROUPS = 2
SSD_CHUNK = 128
SSD_CONV_CH = SSD_INNER + 2 * SSD_GROUPS * SSD_N
SSD_COLS = SSD_INNER + SSD_CONV_CH + SSD_HEADS
LRU_WIDTH = MIX_WIDTH - SSD_INNER
LRU_BLOCKS = 8
LRU_BD = LRU_WIDTH // LRU_BLOCKS
LRU_C = 8.0
CONV_W = 4
PEER_HEADS = 8
N_KEYS = 128
PEER_TOPK = 16
PEER_DQ = 256
PEER_BLOCK = 128

VMEM_LIMIT_BYTES = 56 * 1024 * 1024


def _mm_kernel(a_ref, b_ref, o_ref):
    o_ref[...] = jnp.dot(a_ref[...].astype(jnp.bfloat16), b_ref[...].astype(jnp.bfloat16),
                         preferred_element_type=jnp.float32)


def pmm(a, b, tm=256):
    M, K = a.shape
    N = b.shape[1]
    assert M % tm == 0
    return pl.pallas_call(
        _mm_kernel,
        grid=(M // tm,),
        in_specs=[pl.BlockSpec((tm, K), lambda i: (i, 0)), pl.BlockSpec((K, N), lambda i: (0, 0))],
        out_specs=pl.BlockSpec((tm, N), lambda i: (i, 0)),
        out_shape=jax.ShapeDtypeStruct((M, N), jnp.float32),
        compiler_params=pltpu.CompilerParams(dimension_semantics=("arbitrary",),
                                             vmem_limit_bytes=VMEM_LIMIT_BYTES),
        name="pmm",
    )(a, b)


SUBLANES = 8
LANES = 128
PEER_E = PEER_HEADS * PEER_TOPK
PEER_TB = 8
PEER_SLOTS = 3


def _fold8(p):
    sub = lax.broadcasted_iota(jnp.int32, (SUBLANES, LANES), 0)

    def comb(a, b, k):
        fa = a + pltpu.roll(a, SUBLANES - k, 0)
        fb = b + pltpu.roll(b, k, 0)
        return jnp.where((sub & k) == 0, fa, fb)

    return comb(comb(comb(p[0], p[4], 4), comb(p[2], p[6], 4), 2),
                comb(comb(p[1], p[5], 4), comb(p[3], p[7], 4), 2), 1)


def _peer_kernel(idx_cur_ref, idx_nx1_ref, idx_nxt_ref, gate_ref, x_ref, tab_ref, o_ref, buf, sem):
    i = pl.program_id(0)
    n = pl.num_programs(0)
    slot = i % PEER_SLOTS
    nslot = (i + PEER_SLOTS - 1) % PEER_SLOTS

    def issue(idx_ref, t, s, e0=0, e1=PEER_E):
        for e in range(e0, e1):
            j = t * PEER_E + e
            pltpu.make_async_copy(tab_ref.at[idx_ref[0, 0, j]], buf.at[s, j], sem.at[s]).start(priority=e % 2)

    def wait_slot(s):
        pltpu.make_async_copy(buf.at[s], buf.at[s], sem.at[s]).wait()

    @pl.when(i == 0)
    def _():
        lax.fori_loop(0, PEER_TB, lambda t, c: (issue(idx_cur_ref, t, 0), c)[1], 0)
        lax.fori_loop(0, PEER_TB, lambda t, c: (issue(idx_nx1_ref, t, 1), c)[1], 0)

    wait_slot(slot)

    n_groups = PEER_E // SUBLANES
    per_group = PEER_E // (2 * n_groups)
    n_acc = 4

    def token(t, carry):
        xt = x_ref[t]
        base = t * PEER_E
        tok = (i % (ROUTE_TM // PEER_TB)) * PEER_TB + t
        onehot = (lax.broadcasted_iota(jnp.int32, (SUBLANES, ROUTE_TM), 1) == tok).astype(jnp.float32)
        coefs = []
        for g in range(n_groups):
            rows = [buf[slot, base + g * SUBLANES + j, 0] for j in range(SUBLANES)]
            issue(idx_nxt_ref, t, nslot, g * per_group, (g + 1) * per_group)
            a = jnp.sum(_fold8([r * xt for r in rows]), axis=1, keepdims=True)
            gate = jnp.sum(gate_ref[0, pl.ds(g * SUBLANES, SUBLANES), :] * onehot, axis=1, keepdims=True)
            coefs.append(jnp.broadcast_to(jax.nn.gelu(a) * gate, (SUBLANES, LANES)))
        accs = [jnp.zeros((SUBLANES, LANES), jnp.float32) for _ in range(n_acc)]
        for g in range(n_groups):
            rows = [buf[slot, base + g * SUBLANES + j, 1] for j in range(SUBLANES)]
            issue(idx_nxt_ref, t, nslot, (n_groups + g) * per_group, (n_groups + g + 1) * per_group)
            for j in range(SUBLANES):
                accs[j % n_acc] = accs[j % n_acc] + coefs[g][j:j + 1, :] * rows[j]
        o_ref[t] = (accs[0] + accs[1]) + (accs[2] + accs[3])
        return carry

    lax.fori_loop(0, PEER_TB, token, 0)

    @pl.when(i == n - 1)
    def _():
        wait_slot((i + 1) % PEER_SLOTS)
        wait_slot(nslot)


def peer_experts(xf, idx, gate_t, u_tab, v_tab):
    M, D = xf.shape
    NE = u_tab.shape[0]
    assert D == SUBLANES * LANES and M % ROUTE_TM == 0 and ROUTE_TM % PEER_TB == 0 and idx.shape == (M, PEER_E)
    assert gate_t.shape == (M // ROUTE_TM, PEER_E, ROUTE_TM)
    nblk = M // PEER_TB
    tab = jnp.stack([u_tab.reshape(NE, SUBLANES, LANES), v_tab.reshape(NE, SUBLANES, LANES)], axis=1)
    idx3 = idx.astype(jnp.int32).reshape(nblk, 1, PEER_TB * PEER_E)
    x3 = xf.reshape(M, SUBLANES, LANES)
    smem_blk = (1, 1, PEER_TB * PEER_E)
    out = pl.pallas_call(
        _peer_kernel,
        grid=(nblk,),
        in_specs=[
            pl.BlockSpec(smem_blk, lambda i: (i, 0, 0), memory_space=pltpu.SMEM),
            pl.BlockSpec(smem_blk, lambda i: (jnp.minimum(i + 1, nblk - 1), 0, 0), memory_space=pltpu.SMEM),
            pl.BlockSpec(smem_blk, lambda i: (jnp.minimum(i + 2, nblk - 1), 0, 0), memory_space=pltpu.SMEM),
            pl.BlockSpec((1, PEER_E, ROUTE_TM), lambda i: (i // (ROUTE_TM // PEER_TB), 0, 0)),
            pl.BlockSpec((PEER_TB, SUBLANES, LANES), lambda i: (i, 0, 0)),
            pl.BlockSpec(memory_space=pl.ANY),
        ],
        out_specs=pl.BlockSpec((PEER_TB, SUBLANES, LANES), lambda i: (i, 0, 0)),
        out_shape=jax.ShapeDtypeStruct((M, SUBLANES, LANES), jnp.float32),
        scratch_shapes=[
            pltpu.VMEM((PEER_SLOTS, PEER_TB * PEER_E, 2, SUBLANES, LANES), jnp.float32),
            pltpu.SemaphoreType.DMA((PEER_SLOTS,)),
        ],
        compiler_params=pltpu.CompilerParams(dimension_semantics=("arbitrary",),
                                             vmem_limit_bytes=VMEM_LIMIT_BYTES),
        name="peer_experts",
    )(idx3, idx3, idx3, gate_t.astype(jnp.float32), x3, tab)
    return out.reshape(M, D)


ROUTE_TM = 128
PEER_HALF = PEER_DQ // 2


def _topk_rows(s, k, rank=None, payload=None):
    if rank is None:
        rank = lax.broadcasted_iota(jnp.int32, s.shape, 0)
    big = jnp.iinfo(jnp.int32).max
    vals, sel = [], []
    for _ in range(k):
        m = jnp.max(s, axis=0, keepdims=True)
        pos = jnp.min(jnp.where(s == m, rank, big), axis=0, keepdims=True)
        hit = rank == pos
        vals.append(m)
        sel.append(pos if payload is None else jnp.max(jnp.where(hit, payload, -1), axis=0, keepdims=True))
        s = jnp.where(hit, -jnp.inf, s)
    return jnp.concatenate(vals, axis=0), jnp.concatenate(sel, axis=0)


def _product_candidates(top_s, top_i):
    K = PEER_TOPK
    s1, s2 = top_s
    i1, i2 = top_i
    TM = s1.shape[1]
    vals, flat, eid = [], [], []
    for i in range(2):
        nj = K // (i + 1)
        j = lax.broadcasted_iota(jnp.int32, (nj, TM), 0)
        vals.append(s1[i:i + 1, :] + s2[:nj, :])
        flat.append(i * K + j)
        eid.append(i1[i:i + 1, :] * N_KEYS + i2[:nj, :])
    for j in range(K // 3):
        ni_end = K // (j + 1)
        rows = K if ni_end > SUBLANES else SUBLANES
        i = lax.broadcasted_iota(jnp.int32, (rows, TM), 0)
        ok = (i >= 2) & (i < ni_end)
        vals.append(jnp.where(ok, s1[:rows, :] + s2[j:j + 1, :], -jnp.inf))
        flat.append(i * K + j)
        eid.append(i1[:rows, :] * N_KEYS + i2[j:j + 1, :])
    cat = lambda xs: jnp.concatenate(xs, axis=0)
    return cat(vals), cat(flat), cat(eid)


def _route_kernel(x_ref, wq_ref, keys_ref, idx_ref, gate_ref):
    q = jnp.dot(x_ref[...].astype(jnp.bfloat16), wq_ref[...], preferred_element_type=jnp.float32)
    for h in range(PEER_HEADS):
        top_s, top_i = [], []
        for half in range(2):
            c0 = (h * 2 + half) * PEER_HALF
            qh = q[:, c0:c0 + PEER_HALF].astype(jnp.bfloat16)
            s = lax.dot_general(keys_ref[h, half], qh, (((1,), (1,)), ((), ())),
                                preferred_element_type=jnp.float32)
            ts, ti = _topk_rows(s, PEER_TOPK)
            top_s.append(ts)
            top_i.append(ti)
        cand_s, cand_flat, cand_i = _product_candidates(top_s, top_i)
        best_s, best_i = _topk_rows(cand_s, PEER_TOPK, cand_flat, cand_i)
        e = jnp.exp(best_s - best_s[0:1, :])
        r0 = h * PEER_TOPK
        gate_ref[0, r0:r0 + PEER_TOPK, :] = e / jnp.sum(e, axis=0, keepdims=True)
        idx_ref[0, r0:r0 + PEER_TOPK, :] = best_i


def peer_route(xf, wq, keys):
    M, D = xf.shape
    assert M % ROUTE_TM == 0
    nblk = M // ROUTE_TM
    out_blk = pl.BlockSpec((1, PEER_E, ROUTE_TM), lambda i: (i, 0, 0))
    idx, gate = pl.pallas_call(
        _route_kernel,
        grid=(nblk,),
        in_specs=[pl.BlockSpec((ROUTE_TM, D), lambda i: (i, 0)),
                  pl.BlockSpec(wq.shape, lambda i: (0, 0)),
                  pl.BlockSpec(keys.shape, lambda i: (0, 0, 0, 0))],
        out_specs=[out_blk, out_blk],
        out_shape=[jax.ShapeDtypeStruct((nblk, PEER_E, ROUTE_TM), jnp.int32),
                   jax.ShapeDtypeStruct((nblk, PEER_E, ROUTE_TM), jnp.float32)],
        compiler_params=pltpu.CompilerParams(dimension_semantics=("arbitrary",),
                                             vmem_limit_bytes=VMEM_LIMIT_BYTES),
        name="peer_route",
    )(xf, wq.astype(jnp.bfloat16), keys.astype(jnp.bfloat16))
    return idx.transpose(0, 2, 1).reshape(M, PEER_E), gate


def _mm_parts_kernel(*refs):
    n = (len(refs) - 1) // 2
    acc = None
    for a_ref, b_ref in zip(refs[:n], refs[n:2 * n]):
        part = jnp.dot(a_ref[...].astype(jnp.bfloat16), b_ref[...].astype(jnp.bfloat16),
                       preferred_element_type=jnp.float32)
        acc = part if acc is None else acc + part
    refs[-1][...] = acc


def pmm_parts(parts, w, tm=256):
    M = parts[0].shape[0]
    N = w.shape[1]
    assert M % tm == 0 and sum(p.shape[1] for p in parts) == w.shape[0]
    offs = np.cumsum([0] + [p.shape[1] for p in parts])
    ws = [w[int(offs[j]):int(offs[j + 1])] for j in range(len(parts))]
    return pl.pallas_call(
        _mm_parts_kernel,
        grid=(M // tm,),
        in_specs=[pl.BlockSpec((tm, p.shape[1]), lambda i: (i, 0)) for p in parts]
                 + [pl.BlockSpec(wj.shape, lambda i: (0, 0)) for wj in ws],
        out_specs=pl.BlockSpec((tm, N), lambda i: (i, 0)),
        out_shape=jax.ShapeDtypeStruct((M, N), jnp.float32),
        compiler_params=pltpu.CompilerParams(dimension_semantics=("arbitrary",),
                                             vmem_limit_bytes=VMEM_LIMIT_BYTES),
        name="pmm_parts",
    )(*parts, *ws)


def mm3(x, w):
    if isinstance(x, tuple):
        B, T, _ = x[0].shape
        return pmm_parts([p.reshape(B * T, p.shape[-1]) for p in x], w).reshape(B, T, w.shape[1])
    B, T, K = x.shape
    return pmm(x.reshape(B * T, K), w).reshape(B, T, w.shape[1])


def rmsnorm(x, g, eps=1e-6):
    xf = x.astype(jnp.float32)
    y = xf * lax.rsqrt(jnp.mean(xf * xf, -1, keepdims=True) + eps)
    return (y * g.astype(jnp.float32)).astype(x.dtype)


def modulate(x, g, shift, scale):
    return rmsnorm(x, g) * (1 + scale) + shift


def split_cols(p, sizes):
    return jnp.split(p, [int(i) for i in np.cumsum(sizes)], axis=-1)


def flip(t):
    return jnp.flip(t, axis=1)


def grid_to_cols(x):
    B, T, C = x.shape
    rows = T // GRID_W
    return x.reshape(B, rows, GRID_W, C).transpose(0, 2, 1, 3).reshape(B, T, C)


def cols_to_grid(x):
    B, T, C = x.shape
    rows = T // GRID_W
    return x.reshape(B, GRID_W, rows, C).transpose(0, 2, 1, 3).reshape(B, T, C)


def qshift_grid(x):
    B, T, C = x.shape
    rows = T // GRID_W
    g = jnp.pad(x.reshape(B, rows, GRID_W, C), ((0, 0), (1, 1), (1, 1), (0, 0)))
    sel = jnp.arange(C) % 4
    out = jnp.where(sel == 0, g[:, 1:-1, :-2], jnp.where(sel == 1, g[:, 1:-1, 2:],
                    jnp.where(sel == 2, g[:, :-2, 1:-1], g[:, 2:, 1:-1])))
    return out.reshape(B, T, C)


def shift_seq(x):
    prev = jnp.pad(x, ((0, 0), (1, 0), (0, 0)))[:, :-1]
    nxt = jnp.pad(x, ((0, 0), (0, 1), (0, 0)))[:, 1:]
    return jnp.where(jnp.arange(x.shape[-1]) % 2 == 0, prev, nxt)


def dwconv(x, w, b):
    lo, hi = (CONV_W - 1) // 2, CONV_W // 2
    T = x.shape[1]
    xp = jnp.pad(x, ((0, 0), (lo, hi), (0, 0)))
    y = sum(xp[:, j:j + T] * w[j].astype(x.dtype) for j in range(CONV_W))
    return y + b


RW_L = 64
RW_SUB = 16
RW_PAIRS = RW_HEADS // 2


def _bdot(a, b):
    return jnp.dot(a.astype(jnp.bfloat16), b.astype(jnp.bfloat16), preferred_element_type=jnp.float32)


def _bdot_nt(a, b):
    return lax.dot_general(a.astype(jnp.bfloat16), b.astype(jnp.bfloat16), (((1,), (1,)), ((), ())),
                           preferred_element_type=jnp.float32)


def _rwkv_chunk(lws, rs, ks, vs, kks, as_, states, reverse):
    L, L2 = RW_L, 2 * RW_L
    P = range(len(lws))
    row = lax.broadcasted_iota(jnp.int32, (L, L), 0)
    col = lax.broadcasted_iota(jnp.int32, (L, L), 1)
    before = (lambda a, b: a > b) if reverse else (lambda a, b: a < b)
    tri = jnp.logical_not(before(row, col)).astype(jnp.float32)
    row2 = lax.broadcasted_iota(jnp.int32, (L2, L2), 0)
    col2 = lax.broadcasted_iota(jnp.int32, (L2, L2), 1)
    same = (row2 // L) == (col2 // L)
    strict = same & before(col2, row2)
    incl = same & jnp.logical_not(before(row2, col2))
    diag_blk = (row2 // RW_SUB) == (col2 // RW_SUB)
    eye = (row2 == col2).astype(jnp.float32)
    head_a = (lax.broadcasted_iota(jnp.int32, (1, LANES), 1) // RW_HD) == 0

    def stack2(x):
        return jnp.concatenate([jnp.where(head_a, x, 0.0), jnp.where(head_a, 0.0, x)], axis=0)

    cum = [jnp.dot(tri, lws[p], precision=lax.Precision.HIGHEST, preferred_element_type=jnp.float32) for p in P]
    last = 0 if reverse else L - 1
    cum_l = [cum[p][last:last + 1, :] for p in P]
    b = [as_[p] * kks[p] for p in P]
    ginv = [jnp.exp(-cum[p]) for p in P]
    gtail = [jnp.exp(cum_l[p] - cum[p]) for p in P]
    left = [jnp.concatenate([stack2(kks[p] * jnp.exp(cum[p] - lws[p])), stack2(rs[p] * jnp.exp(cum[p]))], axis=0)
            for p in P]
    right = [jnp.concatenate([stack2(b[p] * ginv[p]), stack2(ks[p] * ginv[p])], axis=0) for p in P]
    v2 = [stack2(vs[p]) for p in P]
    g = [_bdot_nt(left[p], right[p]) for p in P]
    wy = [_bdot_nt(left[p], states[p]) for p in P]
    n = [jnp.where(strict, g[p][:L2, :L2], 0.0) for p in P]
    mkk = [jnp.where(strict, g[p][:L2, L2:], 0.0) for p in P]
    mrb = [jnp.where(incl, g[p][L2:, :L2], 0.0) for p in P]
    mrk = [jnp.where(incl, g[p][L2:, L2:], 0.0) for p in P]
    d = [jnp.where(diag_blk, n[p], 0.0) for p in P]
    d2 = [_bdot(d[p], d[p]) for p in P]
    d4 = [_bdot(d2[p], d2[p]) for p in P]
    d8 = [_bdot(d4[p], d4[p]) for p in P]
    tinv = [_bdot(eye - d[p], eye + d2[p]) for p in P]
    tinv = [_bdot(tinv[p], eye + d4[p]) for p in P]
    tinv = [_bdot(tinv[p], eye + d8[p]) for p in P]
    rhs = [wy[p][:L2] + _bdot(mkk[p], v2[p]) for p in P]

    nblk = L // RW_SUB
    zero_blk = jnp.zeros((RW_SUB, LANES), jnp.float32)

    def pick(x, blk):
        lo = blk * RW_SUB
        return jnp.concatenate([x[lo:lo + RW_SUB], x[L + lo:L + lo + RW_SUB]], axis=0)

    def place(blocks):
        half = lambda h: [zero_blk if blocks[j] is None else blocks[j][h * RW_SUB:(h + 1) * RW_SUB]
                          for j in range(nblk)]
        return jnp.concatenate(half(0) + half(1), axis=0)

    u_blocks = [[None] * nblk for _ in P]
    order = list(reversed(range(nblk))) if reverse else list(range(nblk))
    for blk in order:
        for p in P:
            acc = pick(rhs[p], blk)
            if blk != order[0]:
                acc = acc - _bdot(pick(n[p], blk), place(u_blocks[p]))
            cur = [None] * nblk
            cur[blk] = acc
            u_blocks[p][blk] = _bdot(pick(tinv[p], blk), place(cur))
    u = [place(u_blocks[p]) for p in P]
    y2 = [wy[p][L2:] - _bdot(mrb[p], u[p]) + _bdot(mrk[p], v2[p]) for p in P]
    ys = [y2[p][:L] + y2[p][L:] for p in P]
    upd = [_bdot(v2[p].T, stack2(ks[p] * gtail[p])) - _bdot(u[p].T, stack2(b[p] * gtail[p])) for p in P]
    s_news = [states[p] * jnp.exp(cum_l[p]) + upd[p] for p in P]
    return ys, s_news


def _rwkv_kernel(lw_ref, r_ref, k_ref, v_ref, kk_ref, a_ref, s0_ref, y_ref, sT_ref, s_scr, *, reverse):
    c = pl.program_id(1)

    @pl.when(c == 0)
    def _():
        s_scr[...] = s0_ref[0]

    slabs = [pl.ds(p * LANES, LANES) for p in range(RW_PAIRS)]
    ys, s_news = _rwkv_chunk(*[[ref[0, :, sl] for sl in slabs]
                               for ref in (lw_ref, r_ref, k_ref, v_ref, kk_ref, a_ref)],
                             [s_scr[p] for p in range(RW_PAIRS)], reverse)
    for p in range(RW_PAIRS):
        y_ref[0, :, slabs[p]] = ys[p]
        s_scr[p] = s_news[p]

    @pl.when(c == pl.num_programs(1) - 1)
    def _():
        sT_ref[0] = s_scr[...]


def rwkv7_scan(rkv, lw, k, kk, a, S0, reverse=False):
    B, T, C = lw.shape
    assert C == RW_PAIRS * LANES and T % RW_L == 0 and rkv.shape[-1] >= 3 * C
    nc = T // RW_L
    chunk = (lambda c: nc - 1 - c) if reverse else (lambda c: c)
    seq = pl.BlockSpec((1, RW_L, C), lambda b, c: (b, chunk(c), 0))
    r_spec = seq
    v_spec = pl.BlockSpec((1, RW_L, C), lambda b, c: (b, chunk(c), 2))
    st = pl.BlockSpec((1, RW_PAIRS, LANES, LANES), lambda b, c: (b, 0, 0, 0))
    y, sT = pl.pallas_call(
        functools.partial(_rwkv_kernel, reverse=reverse),
        grid=(B, nc),
        in_specs=[seq, r_spec, seq, v_spec, seq, seq, st],
        out_specs=[seq, st],
        out_shape=[jax.ShapeDtypeStruct((B, T, C), jnp.float32),
                   jax.ShapeDtypeStruct((B, RW_PAIRS, LANES, LANES), jnp.float32)],
        scratch_shapes=[pltpu.VMEM((RW_PAIRS, LANES, LANES), jnp.float32)],
        compiler_params=pltpu.CompilerParams(dimension_semantics=("arbitrary", "arbitrary"),
                                             vmem_limit_bytes=VMEM_LIMIT_BYTES),
        name="rwkv7_scan",
    )(lw, rkv, k, rkv, kk, a, S0)
    return y, sT


def rwkv_stream(p, S0, w0, w_up, a0, a_up, g_up, k_k, k_a, r_k, ln_g, ln_b):
    B, T, _ = p.shape
    r, k, v, wd, ad, gd = split_cols(p, [RW_WIDTH, RW_WIDTH, RW_WIDTH, RW_DECAY_RANK, RW_A_RANK])
    heads = lambda t: t.reshape(B, T, RW_HEADS, RW_HD)
    kk = heads(k * k_k).astype(jnp.float32)
    kk = (kk * lax.rsqrt(jnp.maximum(jnp.sum(kk * kk, -1, keepdims=True), 1e-12))).reshape(B, T, RW_WIDTH)
    g = jax.nn.sigmoid(gd) @ g_up
    per_dir, bonus = [], []
    for d in range(2):
        wl = (w0[d] + jnp.tanh(wd) @ w_up[d]).astype(jnp.float32)
        lw = -jnp.exp(-jax.nn.softplus(-wl) - 0.5)
        a = jax.nn.sigmoid(a0[d] + ad @ a_up[d])
        kd = k * (1 + (a - 1) * k_a)
        per_dir.append(rwkv7_scan(p, lw, kd, kk, a, S0[d], reverse=(d == 1)))
        bonus.append(jnp.sum(heads(r) * heads(kd) * r_k, -1, keepdims=True) * heads(v))
    sT = (per_dir[0][1], per_dir[1][1])
    y = (per_dir[0][0] + per_dir[1][0]).astype(jnp.float32).reshape(B, T, RW_HEADS, RW_HD)
    mu = jnp.mean(y, -1, keepdims=True)
    var = jnp.mean(jnp.square(y - mu), -1, keepdims=True)
    yn = ((y - mu) * lax.rsqrt(var + RW_GN_EPS)).reshape(B, T, RW_WIDTH) * ln_g + ln_b
    out = (yn + (bonus[0] + bonus[1]).reshape(B, T, RW_WIDTH)) * g
    return out.astype(p.dtype), sT


GLA_SUB = 16
GLA_PAIRS = GLA_HEADS // 2
assert GLA_DV == LANES and 2 * GLA_DK == LANES


def _gla_kernel(q_ref, k_ref, g_ref, v_ref, s0_ref, o_ref, sT_ref, s_scr, *, reverse):
    c = pl.program_id(1)
    L = GLA_CHUNK

    @pl.when(c == 0)
    def _():
        s_scr[...] = s0_ref[0]

    row = lax.broadcasted_iota(jnp.int32, (L, L), 0)
    col = lax.broadcasted_iota(jnp.int32, (L, L), 1)
    incl = (col >= row) if reverse else (col <= row)
    last = 0 if reverse else L - 1
    lane = lax.broadcasted_iota(jnp.int32, (1, LANES), 1)
    trow = lax.broadcasted_iota(jnp.int32, (L, LANES), 0)
    outs = []
    for p in range(GLA_PAIRS):
        sl = pl.ds(p * LANES, LANES)
        q, k, g = q_ref[0, :, sl], k_ref[0, :, sl], g_ref[0, :, sl]
        b = jnp.dot(incl.astype(jnp.float32), g, precision=lax.Precision.HIGHEST,
                    preferred_element_type=jnp.float32)
        b_l = b[last:last + 1, :]
        qe = q * jnp.exp(b)
        kd = k * jnp.exp(b_l - b)
        qfs, kfs = [], []
        for blk in range(L // GLA_SUB):
            lo, hi = blk * GLA_SUB, (blk + 1) * GLA_SUB
            first = hi - 1 if reverse else lo
            b_ref = b[first:first + 1, :]
            qfs.append(q[lo:hi, :] * jnp.exp(b[lo:hi, :] - b_ref))
            used = (trow >= lo) if reverse else (trow < hi)
            kfs.append(jnp.where(used, k * jnp.exp(jnp.where(used, b_ref - b, 0.0)), 0.0))
        for hh in range(2):
            h = 2 * p + hh
            m = (lane // GLA_DK) == hh
            v = v_ref[0, :, pl.ds(h * GLA_DV, GLA_DV)]
            s_t = s_scr[h]
            a = jnp.concatenate([_bdot_nt(qf, jnp.where(m, kf, 0.0)) for qf, kf in zip(qfs, kfs)], axis=0)
            o = _bdot(jnp.where(incl, a, 0.0), v) + _bdot_nt(qe, s_t)
            s_new = s_t * jnp.exp(b_l) + _bdot(v.T, jnp.where(m, kd, 0.0))
            outs.append((h, o, s_new))
    for h, o, s_new in outs:
        o_ref[0, :, pl.ds(h * GLA_DV, GLA_DV)] = o
        s_scr[h] = s_new

    @pl.when(c == pl.num_programs(1) - 1)
    def _():
        sT_ref[0] = s_scr[...]


def gla_chunked(q, k, g, v, S0, reverse=False):
    B, T, _ = q.shape
    assert T % GLA_CHUNK == 0
    nc = T // GLA_CHUNK
    chunk = (lambda b, c: (b, nc - 1 - c, 0)) if reverse else (lambda b, c: (b, c, 0))
    qk = pl.BlockSpec((1, GLA_CHUNK, GLA_HEADS * GLA_DK), chunk)
    vv = pl.BlockSpec((1, GLA_CHUNK, GLA_HEADS * GLA_DV), chunk)
    st = pl.BlockSpec((1, GLA_HEADS, LANES, LANES), lambda b, c: (b, 0, 0, 0))
    return pl.pallas_call(
        functools.partial(_gla_kernel, reverse=reverse),
        grid=(B, nc),
        in_specs=[qk, qk, qk, vv, st],
        out_specs=[vv, st],
        out_shape=[jax.ShapeDtypeStruct((B, T, GLA_HEADS * GLA_DV), jnp.float32),
                   jax.ShapeDtypeStruct((B, GLA_HEADS, LANES, LANES), jnp.float32)],
        scratch_shapes=[pltpu.VMEM((GLA_HEADS, LANES, LANES), jnp.float32)],
        compiler_params=pltpu.CompilerParams(dimension_semantics=("arbitrary", "arbitrary"),
                                             vmem_limit_bytes=VMEM_LIMIT_BYTES),
        name="gla_chunked",
    )(q, k, g, v, S0)


def gla_stream(p, S0, gate_up, gate_b, norm_g):
    B, T, _ = p.shape
    q, k, v, gd, og = split_cols(p, [GLA_HEADS * GLA_DK, GLA_HEADS * GLA_DK, GLA_HEADS * GLA_DV, GLA_GATE_RANK])
    q = q * GLA_DK ** -0.5
    outs, states = [], []
    for d in range(2):
        glog = jax.nn.log_sigmoid((gd @ gate_up[d] + gate_b[d]).astype(jnp.float32)) / GLA_GATE_NORM
        od, sd = gla_chunked(q, k, glog, v, S0[d], reverse=(d == 1))
        outs.append(od)
        states.append(sd)
    o = (outs[0] + outs[1]).reshape(B, T, GLA_HEADS, GLA_DV)
    o = rmsnorm(o, norm_g.reshape(GLA_HEADS, GLA_DV)).reshape(B, T, GLA_HEADS * GLA_DV)
    return (o * jax.nn.silu(og)).astype(p.dtype), tuple(states)


def ssd_chunked(x, dt, A, Bm, Cm, S0):
    Bsz, T, H, P = x.shape
    N = Bm.shape[-1]
    L = SSD_CHUNK
    n = T // L
    f = lambda t: t.astype(jnp.float32)
    xc = (f(x) * f(dt)[..., None]).reshape(Bsz, n, L, H, P)
    Bc = f(Bm).reshape(Bsz, n, L, H, N)
    Cc = f(Cm).reshape(Bsz, n, L, H, N)
    acs = jnp.cumsum((f(dt) * A).reshape(Bsz, n, L, H).transpose(0, 3, 1, 2), -1)
    mask = jnp.tril(jnp.ones((L, L), bool))
    Lmat = jnp.exp(jnp.where(mask, acs[..., :, None] - acs[..., None, :], -jnp.inf))
    G = jnp.einsum("bclhn,bcshn->bhcls", Cc, Bc) * Lmat
    y_diag = jnp.einsum("bhcls,bcshp->bclhp", G, xc)
    contrib = jnp.einsum("bclhn,bhcl,bclhp->bchpn", Bc, jnp.exp(acs[..., -1:] - acs), xc)
    chunk_decay = jnp.exp(acs[..., -1])

    def step(S, inp):
        dec, st = inp
        return dec[..., None, None] * S + st, S

    S, S_in = lax.scan(step, S0, (jnp.moveaxis(chunk_decay, 2, 0), jnp.moveaxis(contrib, 1, 0)))
    y_off = jnp.einsum("bclhn,bchpn,bhcl->bclhp", Cc, jnp.moveaxis(S_in, 0, 1), jnp.exp(acs))
    return (y_diag + y_off).reshape(Bsz, T, H, P).astype(x.dtype), S


def ssd_stream(p, S0f, S0b, conv_w, conv_b, dt_bias, A_log, D_skip, norm_g):
    B, T, _ = p.shape
    z, xbc, dtr = split_cols(p, [SSD_INNER, SSD_CONV_CH])
    xbc = jax.nn.silu(dwconv(xbc, conv_w, conv_b))
    xs, Bm, Cm = split_cols(xbc, [SSD_INNER, SSD_GROUPS * SSD_N])
    xs = xs.reshape(B, T, SSD_HEADS, SSD_P)
    rep = SSD_HEADS // SSD_GROUPS
    Bm = jnp.repeat(Bm.reshape(B, T, SSD_GROUPS, SSD_N), rep, axis=2)
    Cm = jnp.repeat(Cm.reshape(B, T, SSD_GROUPS, SSD_N), rep, axis=2)
    ys, states = [], []
    for d, S0 in enumerate((S0f, S0b)):
        dt = jax.nn.softplus((dtr + dt_bias[d]).astype(jnp.float32))
        A = -jnp.exp(A_log[d].astype(jnp.float32))
        args = [xs, dt, Bm, Cm]
        if d == 1:
            args = [flip(t) for t in args]
        yd, Sd = ssd_chunked(args[0], args[1], A, args[2], args[3], S0)
        ys.append(flip(yd) if d == 1 else yd)
        states.append(Sd)
    y = (ys[0] + ys[1] + xs * D_skip[:, None]).reshape(B, T, SSD_INNER) * jax.nn.silu(z)
    return rmsnorm(y, norm_g).astype(p.dtype), states[0], states[1]


def linear_scan(a, b, h0):
    b = b.at[:, 0].add(a[:, 0] * h0)
    comb = lambda l, r: (l[0] * r[0], r[0] * l[1] + r[1])
    _, h = lax.associative_scan(comb, (a, b), axis=1)
    return h, h[:, -1]


def lru_stream(p, h0f, h0b, conv_w, conv_b, wa, ba, wx, bx, lam):
    B, T, _ = p.shape
    xr, gate = split_cols(p, [LRU_WIDTH])
    xf = dwconv(xr, conv_w, conv_b).astype(jnp.float32)
    xh = xf.reshape(B, T, LRU_BLOCKS, LRU_BD)
    hs, finals = [], []
    for d, h0 in enumerate((h0f, h0b)):
        r = jax.nn.sigmoid(jnp.einsum("btgi,gij->btgj", xh, wa[d]).reshape(B, T, LRU_WIDTH) + ba[d])
        i = jax.nn.sigmoid(jnp.einsum("btgi,gij->btgj", xh, wx[d]).reshape(B, T, LRU_WIDTH) + bx[d])
        log_a = -LRU_C * r * jax.nn.softplus(-lam[d])
        a = jnp.exp(log_a)
        b = jnp.sqrt(-jnp.expm1(2 * log_a)) * (i * xf)
        if d == 1:
            a, b = flip(a), flip(b)
        h, hT = linear_scan(a, b, h0)
        hs.append(flip(h) if d == 1 else h)
        finals.append(hT)
    y = (hs[0] + hs[1]).astype(p.dtype) * jax.nn.gelu(gate)
    return y, finals[0], finals[1]


def even_mixer(u, uc, w_in, mu, rw_p, gla_p):
    B = u.shape[0]
    p, pc = mm3(u, w_in), mm3(uc, w_in)
    rw, gl = p[..., :RW_COLS], p[..., RW_COLS:]
    rwc, glc = pc[..., :RW_COLS], pc[..., RW_COLS:]
    rw = rw + mu * (qshift_grid(rw) - rw)
    rwc = rwc + mu * (shift_seq(rwc) - rwc)
    z_rw = jnp.zeros((B, RW_PAIRS, LANES, LANES), jnp.float32)
    yc_rw, s_ctx = rwkv_stream(rwc, (z_rw, z_rw), *rw_p)
    y_rw, _ = rwkv_stream(rw, s_ctx, *rw_p)
    z_gla = jnp.zeros((B, GLA_HEADS, LANES, LANES), jnp.float32)
    yc_gl, g_ctx = gla_stream(glc, (z_gla, z_gla), *gla_p)
    y_gl, _ = gla_stream(gl, g_ctx, *gla_p)
    return (y_rw, y_gl), (yc_rw, yc_gl)


def odd_mixer(u, uc, w_in, ssd_p, lru_p):
    B = u.shape[0]
    p = grid_to_cols(mm3(u, w_in))
    pc = mm3(uc, w_in)
    z_s = jnp.zeros((B, SSD_HEADS, SSD_P, SSD_N), jnp.float32)
    yc_s, Sf, Sb = ssd_stream(pc[..., :SSD_COLS], z_s, z_s, *ssd_p)
    y_s, _, _ = ssd_stream(p[..., :SSD_COLS], Sf, Sb, *ssd_p)
    z_l = jnp.zeros((B, LRU_WIDTH), jnp.float32)
    yc_l, hf, hb = lru_stream(pc[..., SSD_COLS:], z_l, z_l, *lru_p)
    y_l, _, _ = lru_stream(p[..., SSD_COLS:], hf, hb, *lru_p)
    return cols_to_grid(jnp.concatenate([y_s, y_l], -1)), jnp.concatenate([yc_s, yc_l], -1)


def peer(x, wq, keys, u_tab, v_tab):
    B, T, D = x.shape
    M = B * T
    xf = x.reshape(M, D)
    idx, gate_t = peer_route(xf, wq, keys)
    return peer_experts(xf, idx, gate_t, u_tab, v_tab).reshape(B, T, D)


def kernel(x, c, ctx, c_ctx, ada_w, ada_b, norm1_g, norm2_g, ev_w_in, rw_mu, rw_w0, rw_w_up, rw_a0, rw_a_up,
           rw_g_up, rw_k_k, rw_k_a, rw_r_k, rw_ln_g, rw_ln_b, gla_gate_up, gla_gate_b, gla_norm_g, od_w_in,
           ssd_conv_w, ssd_conv_b, ssd_dt_bias, ssd_A_log, ssd_D, ssd_norm_g, lru_conv_w, lru_conv_b, lru_wa,
           lru_ba, lru_wx, lru_bx, lru_lam, w_out, peer_wq, peer_keys, peer_u, peer_v, final_g):
    h, hc = x, ctx
    sc, scc = jax.nn.silu(c), jax.nn.silu(c_ctx)
    for l in range(DEPTH):
        last = l == DEPTH - 1
        i = l // 2
        mod = jnp.split((sc @ ada_w[l] + ada_b[l])[:, None, :], 6, axis=-1)
        modc = jnp.split(scc @ ada_w[l] + ada_b[l], 6, axis=-1)
        u = modulate(h, norm1_g[l], mod[0], mod[1])
        uc = modulate(hc, norm1_g[l], modc[0], modc[1])
        if l % 2 == 0:
            rw_p = (rw_w0[i], rw_w_up[i], rw_a0[i], rw_a_up[i], rw_g_up[i], rw_k_k[i], rw_k_a[i], rw_r_k[i],
                    rw_ln_g[i], rw_ln_b[i])
            gla_p = (gla_gate_up[i], gla_gate_b[i], gla_norm_g[i])
            y, yc = even_mixer(u, uc, ev_w_in[i], rw_mu[i], rw_p, gla_p)
        else:
            ssd_p = (ssd_conv_w[i], ssd_conv_b[i], ssd_dt_bias[i], ssd_A_log[i], ssd_D[i], ssd_norm_g[i])
            lru_p = (lru_conv_w[i], lru_conv_b[i], lru_wa[i], lru_ba[i], lru_wx[i], lru_bx[i], lru_lam[i])
            y, yc = odd_mixer(u, uc, od_w_in[i], ssd_p, lru_p)
        h = h + mod[2] * mm3(y, w_out[l])
        u = modulate(h, norm2_g[l], mod[3], mod[4])
        h = h + mod[5] * peer(u, peer_wq[l], peer_keys[l], peer_u[l], peer_v[l])
        if not last:
            hc = hc + modc[2] * mm3(yc, w_out[l])
            uc = modulate(hc, norm2_g[l], modc[3], modc[4])
            hc = hc + modc[5] * peer(uc, peer_wq[l], peer_keys[l], peer_u[l], peer_v[l])
    return rmsnorm(h, final_g)
```

```python
import functools

import jax
import jax.numpy as jnp
from jax import lax
import numpy as np
from jax.experimental import pallas as pl
from jax.experimental.pallas import tpu as pltpu

D_MODEL = 1024
DEPTH = 4
GRID_W = 64
MIX_WIDTH = D_MODEL

RW_WIDTH = MIX_WIDTH // 2
RW_HD = 64
RW_HEADS = RW_WIDTH // RW_HD
RW_DECAY_RANK = 32
RW_A_RANK = 32
RW_GATE_RANK = 64
RW_GN_EPS = 64e-5
RW_COLS = 3 * RW_WIDTH + RW_DECAY_RANK + RW_A_RANK + RW_GATE_RANK
GLA_HEADS = 4
GLA_DV = (MIX_WIDTH - RW_WIDTH) // GLA_HEADS
GLA_DK = GLA_DV // 2
GLA_GATE_RANK = 16
GLA_GATE_NORM = 16.0
GLA_CHUNK = 64
SSD_INNER = MIX_WIDTH // 2
SSD_P = 64
SSD_HEADS = SSD_INNER // SSD_P
SSD_N = 128
SSD_GROUPS = 2
SSD_CHUNK = 128
SSD_CONV_CH = SSD_INNER + 2 * SSD_GROUPS * SSD_N
SSD_COLS = SSD_INNER + SSD_CONV_CH + SSD_HEADS
LRU_WIDTH = MIX_WIDTH - SSD_INNER
LRU_BLOCKS = 8
LRU_BD = LRU_WIDTH // LRU_BLOCKS
LRU_C = 8.0
CONV_W = 4
PEER_HEADS = 8
N_KEYS = 128
PEER_TOPK = 16
PEER_DQ = 256
PEER_BLOCK = 128

VMEM_LIMIT_BYTES = 56 * 1024 * 1024


def _mm_kernel(a_ref, b_ref, o_ref):
    o_ref[...] = jnp.dot(a_ref[...].astype(jnp.bfloat16), b_ref[...].astype(jnp.bfloat16),
                         preferred_element_type=jnp.float32)


def pmm(a, b, tm=256):
    M, K = a.shape
    N = b.shape[1]
    assert M % tm == 0
    return pl.pallas_call(
        _mm_kernel,
        grid=(M // tm,),
        in_specs=[pl.BlockSpec((tm, K), lambda i: (i, 0)), pl.BlockSpec((K, N), lambda i: (0, 0))],
        out_specs=pl.BlockSpec((tm, N), lambda i: (i, 0)),
        out_shape=jax.ShapeDtypeStruct((M, N), jnp.float32),
        compiler_params=pltpu.CompilerParams(dimension_semantics=("arbitrary",),
                                             vmem_limit_bytes=VMEM_LIMIT_BYTES),
        name="pmm",
    )(a, b)


SUBLANES = 8
LANES = 128
PEER_E = PEER_HEADS * PEER_TOPK
PEER_TB = 8
PEER_SLOTS = 3


def _fold8(p):
    sub = lax.broadcasted_iota(jnp.int32, (SUBLANES, LANES), 0)

    def comb(a, b, k):
        fa = a + pltpu.roll(a, SUBLANES - k, 0)
        fb = b + pltpu.roll(b, k, 0)
        return jnp.where((sub & k) == 0, fa, fb)

    return comb(comb(comb(p[0], p[4], 4), comb(p[2], p[6], 4), 2),
                comb(comb(p[1], p[5], 4), comb(p[3], p[7], 4), 2), 1)


def _peer_kernel(idx_cur_ref, idx_nx1_ref, idx_nxt_ref, gate_ref, x_ref, tab_ref, o_ref, buf, sem):
    i = pl.program_id(0)
    n = pl.num_programs(0)
    slot = i % PEER_SLOTS
    nslot = (i + PEER_SLOTS - 1) % PEER_SLOTS

    def issue(idx_ref, t, s, e0=0, e1=PEER_E):
        for e in range(e0, e1):
            j = t * PEER_E + e
            pltpu.make_async_copy(tab_ref.at[idx_ref[0, 0, j]], buf.at[s, j], sem.at[s]).start(priority=e % 2)

    def wait_slot(s):
        pltpu.make_async_copy(buf.at[s], buf.at[s], sem.at[s]).wait()

    @pl.when(i == 0)
    def _():
        lax.fori_loop(0, PEER_TB, lambda t, c: (issue(idx_cur_ref, t, 0), c)[1], 0)
        lax.fori_loop(0, PEER_TB, lambda t, c: (issue(idx_nx1_ref, t, 1), c)[1], 0)

    wait_slot(slot)

    n_groups = PEER_E // SUBLANES
    per_group = PEER_E // (2 * n_groups)
    n_acc = 4

    def token(t, carry):
        xrow = x_ref[pl.ds(t, 1), :]
        xt = jnp.concatenate([xrow[:, s * LANES:(s + 1) * LANES] for s in range(SUBLANES)], axis=0)
        base = t * PEER_E
        tok = (i % (ROUTE_TM // PEER_TB)) * PEER_TB + t
        onehot = (lax.broadcasted_iota(jnp.int32, (SUBLANES, ROUTE_TM), 1) == tok).astype(jnp.float32)
        coefs = []
        for g in range(n_groups):
            rows = [buf[slot, base + g * SUBLANES + j, 0] for j in range(SUBLANES)]
            issue(idx_nxt_ref, t, nslot, g * per_group, (g + 1) * per_group)
            a = jnp.sum(_fold8([r * xt for r in rows]), axis=1, keepdims=True)
            gate = jnp.sum(gate_ref[0, pl.ds(g * SUBLANES, SUBLANES), :] * onehot, axis=1, keepdims=True)
            coefs.append(jnp.broadcast_to(jax.nn.gelu(a) * gate, (SUBLANES, LANES)))
        accs = [jnp.zeros((SUBLANES, LANES), jnp.float32) for _ in range(n_acc)]
        for g in range(n_groups):
            rows = [buf[slot, base + g * SUBLANES + j, 1] for j in range(SUBLANES)]
            issue(idx_nxt_ref, t, nslot, (n_groups + g) * per_group, (n_groups + g + 1) * per_group)
            for j in range(SUBLANES):
                accs[j % n_acc] = accs[j % n_acc] + coefs[g][j:j + 1, :] * rows[j]
        acc = (accs[0] + accs[1]) + (accs[2] + accs[3])
        o_ref[pl.ds(t, 1), :] = jnp.concatenate([acc[s:s + 1, :] for s in range(SUBLANES)], axis=1)
        return carry

    lax.fori_loop(0, PEER_TB, token, 0)

    @pl.when(i == n - 1)
    def _():
        wait_slot((i + 1) % PEER_SLOTS)
        wait_slot(nslot)


def peer_experts(xf, idx, gate_t, u_tab, v_tab):
    M, D = xf.shape
    NE = u_tab.shape[0]
    assert D == SUBLANES * LANES and M % ROUTE_TM == 0 and ROUTE_TM % PEER_TB == 0 and idx.shape == (M, PEER_E)
    assert gate_t.shape == (M // ROUTE_TM, PEER_E, ROUTE_TM)
    nblk = M // PEER_TB
    tab = jnp.stack([u_tab.reshape(NE, SUBLANES, LANES), v_tab.reshape(NE, SUBLANES, LANES)], axis=1)
    idx3 = idx.astype(jnp.int32).reshape(nblk, 1, PEER_TB * PEER_E)
    smem_blk = (1, 1, PEER_TB * PEER_E)
    out = pl.pallas_call(
        _peer_kernel,
        grid=(nblk,),
        in_specs=[
            pl.BlockSpec(smem_blk, lambda i: (i, 0, 0), memory_space=pltpu.SMEM),
            pl.BlockSpec(smem_blk, lambda i: (jnp.minimum(i + 1, nblk - 1), 0, 0), memory_space=pltpu.SMEM),
            pl.BlockSpec(smem_blk, lambda i: (jnp.minimum(i + 2, nblk - 1), 0, 0), memory_space=pltpu.SMEM),
            pl.BlockSpec((1, PEER_E, ROUTE_TM), lambda i: (i // (ROUTE_TM // PEER_TB), 0, 0)),
            pl.BlockSpec((PEER_TB, D), lambda i: (i, 0)),
            pl.BlockSpec(memory_space=pl.ANY),
        ],
        out_specs=pl.BlockSpec((PEER_TB, D), lambda i: (i, 0)),
        out_shape=jax.ShapeDtypeStruct((M, D), jnp.float32),
        scratch_shapes=[
            pltpu.VMEM((PEER_SLOTS, PEER_TB * PEER_E, 2, SUBLANES, LANES), jnp.float32),
            pltpu.SemaphoreType.DMA((PEER_SLOTS,)),
        ],
        compiler_params=pltpu.CompilerParams(dimension_semantics=("arbitrary",),
                                             vmem_limit_bytes=VMEM_LIMIT_BYTES),
        name="peer_experts",
    )(idx3, idx3, idx3, gate_t.astype(jnp.float32), xf, tab)
    return out


ROUTE_TM = 128
PEER_HALF = PEER_DQ // 2


def _topk_rows(s, k, rank=None, payload=None):
    if rank is None:
        rank = lax.broadcasted_iota(jnp.int32, s.shape, 0)
    big = jnp.iinfo(jnp.int32).max
    vals, sel = [], []
    for _ in range(k):
        m = jnp.max(s, axis=0, keepdims=True)
        pos = jnp.min(jnp.where(s == m, rank, big), axis=0, keepdims=True)
        hit = rank == pos
        vals.append(m)
        sel.append(pos if payload is None else jnp.max(jnp.where(hit, payload, -1), axis=0, keepdims=True))
        s = jnp.where(hit, -jnp.inf, s)
    return jnp.concatenate(vals, axis=0), jnp.concatenate(sel, axis=0)


def _product_candidates(top_s, top_i):
    K = PEER_TOPK
    s1, s2 = top_s
    i1, i2 = top_i
    TM = s1.shape[1]
    vals, flat, eid = [], [], []
    for i in range(2):
        nj = K // (i + 1)
        j = lax.broadcasted_iota(jnp.int32, (nj, TM), 0)
        vals.append(s1[i:i + 1, :] + s2[:nj, :])
        flat.append(i * K + j)
        eid.append(i1[i:i + 1, :] * N_KEYS + i2[:nj, :])
    for j in range(K // 3):
        ni_end = K // (j + 1)
        rows = K if ni_end > SUBLANES else SUBLANES
        i = lax.broadcasted_iota(jnp.int32, (rows, TM), 0)
        ok = (i >= 2) & (i < ni_end)
        vals.append(jnp.where(ok, s1[:rows, :] + s2[j:j + 1, :], -jnp.inf))
        flat.append(i * K + j)
        eid.append(i1[:rows, :] * N_KEYS + i2[j:j + 1, :])
    cat = lambda xs: jnp.concatenate(xs, axis=0)
    return cat(vals), cat(flat), cat(eid)


def _route_kernel(x_ref, wq_ref, keys_ref, idx_ref, gate_ref):
    q = jnp.dot(x_ref[...].astype(jnp.bfloat16), wq_ref[...], preferred_element_type=jnp.float32)
    for h in range(PEER_HEADS):
        top_s, top_i = [], []
        for half in range(2):
            c0 = (h * 2 + half) * PEER_HALF
            qh = q[:, c0:c0 + PEER_HALF].astype(jnp.bfloat16)
            s = lax.dot_general(keys_ref[h, half], qh, (((1,), (1,)), ((), ())),
                                preferred_element_type=jnp.float32)
            ts, ti = _topk_rows(s, PEER_TOPK)
            top_s.append(ts)
            top_i.append(ti)
        cand_s, cand_flat, cand_i = _product_candidates(top_s, top_i)
        best_s, best_i = _topk_rows(cand_s, PEER_TOPK, cand_flat, cand_i)
        e = jnp.exp(best_s - best_s[0:1, :])
        r0 = h * PEER_TOPK
        gate_ref[0, r0:r0 + PEER_TOPK, :] = e / jnp.sum(e, axis=0, keepdims=True)
        idx_ref[0, r0:r0 + PEER_TOPK, :] = best_i


def peer_route(xf, wq, keys):
    M, D = xf.shape
    assert M % ROUTE_TM == 0
    nblk = M // ROUTE_TM
    out_blk = pl.BlockSpec((1, PEER_E, ROUTE_TM), lambda i: (i, 0, 0))
    idx, gate = pl.pallas_call(
        _route_kernel,
        grid=(nblk,),
        in_specs=[pl.BlockSpec((ROUTE_TM, D), lambda i: (i, 0)),
                  pl.BlockSpec(wq.shape, lambda i: (0, 0)),
                  pl.BlockSpec(keys.shape, lambda i: (0, 0, 0, 0))],
        out_specs=[out_blk, out_blk],
        out_shape=[jax.ShapeDtypeStruct((nblk, PEER_E, ROUTE_TM), jnp.int32),
                   jax.ShapeDtypeStruct((nblk, PEER_E, ROUTE_TM), jnp.float32)],
        compiler_params=pltpu.CompilerParams(dimension_semantics=("arbitrary",),
                                             vmem_limit_bytes=VMEM_LIMIT_BYTES),
        name="peer_route",
    )(xf, wq.astype(jnp.bfloat16), keys.astype(jnp.bfloat16))
    return idx.transpose(0, 2, 1).reshape(M, PEER_E), gate


def _mm_parts_kernel(*refs):
    n = (len(refs) - 1) // 2
    acc = None
    for a_ref, b_ref in zip(refs[:n], refs[n:2 * n]):
        part = jnp.dot(a_ref[...].astype(jnp.bfloat16), b_ref[...].astype(jnp.bfloat16),
                       preferred_element_type=jnp.float32)
        acc = part if acc is None else acc + part
    refs[-1][...] = acc


def pmm_parts(parts, w, tm=256):
    M = parts[0].shape[0]
    N = w.shape[1]
    assert M % tm == 0 and sum(p.shape[1] for p in parts) == w.shape[0]
    offs = np.cumsum([0] + [p.shape[1] for p in parts])
    ws = [w[int(offs[j]):int(offs[j + 1])] for j in range(len(parts))]
    return pl.pallas_call(
        _mm_parts_kernel,
        grid=(M // tm,),
        in_specs=[pl.BlockSpec((tm, p.shape[1]), lambda i: (i, 0)) for p in parts]
                 + [pl.BlockSpec(wj.shape, lambda i: (0, 0)) for wj in ws],
        out_specs=pl.BlockSpec((tm, N), lambda i: (i, 0)),
        out_shape=jax.ShapeDtypeStruct((M, N), jnp.float32),
        compiler_params=pltpu.CompilerParams(dimension_semantics=("arbitrary",),
                                             vmem_limit_bytes=VMEM_LIMIT_BYTES),
        name="pmm_parts",
    )(*parts, *ws)


def mm3(x, w):
    if isinstance(x, tuple):
        B, T, _ = x[0].shape
        return pmm_parts([p.reshape(B * T, p.shape[-1]) for p in x], w).reshape(B, T, w.shape[1])
    B, T, K = x.shape
    return pmm(x.reshape(B * T, K), w).reshape(B, T, w.shape[1])


def rmsnorm(x, g, eps=1e-6):
    xf = x.astype(jnp.float32)
    y = xf * lax.rsqrt(jnp.mean(xf * xf, -1, keepdims=True) + eps)
    return (y * g.astype(jnp.float32)).astype(x.dtype)


def modulate(x, g, shift, scale):
    return rmsnorm(x, g) * (1 + scale) + shift


def split_cols(p, sizes):
    return jnp.split(p, [int(i) for i in np.cumsum(sizes)], axis=-1)


def flip(t):
    return jnp.flip(t, axis=1)


def grid_to_cols(x):
    B, T, C = x.shape
    rows = T // GRID_W
    return x.reshape(B, rows, GRID_W, C).transpose(0, 2, 1, 3).reshape(B, T, C)


def cols_to_grid(x):
    B, T, C = x.shape
    rows = T // GRID_W
    return x.reshape(B, GRID_W, rows, C).transpose(0, 2, 1, 3).reshape(B, T, C)


def qshift_grid(x):
    B, T, C = x.shape
    rows = T // GRID_W
    g = jnp.pad(x.reshape(B, rows, GRID_W, C), ((0, 0), (1, 1), (1, 1), (0, 0)))
    sel = jnp.arange(C) % 4
    out = jnp.where(sel == 0, g[:, 1:-1, :-2], jnp.where(sel == 1, g[:, 1:-1, 2:],
                    jnp.where(sel == 2, g[:, :-2, 1:-1], g[:, 2:, 1:-1])))
    return out.reshape(B, T, C)


def shift_seq(x):
    prev = jnp.pad(x, ((0, 0), (1, 0), (0, 0)))[:, :-1]
    nxt = jnp.pad(x, ((0, 0), (0, 1), (0, 0)))[:, 1:]
    return jnp.where(jnp.arange(x.shape[-1]) % 2 == 0, prev, nxt)


def dwconv(x, w, b):
    lo, hi = (CONV_W - 1) // 2, CONV_W // 2
    T = x.shape[1]
    xp = jnp.pad(x, ((0, 0), (lo, hi), (0, 0)))
    y = sum(xp[:, j:j + T] * w[j].astype(x.dtype) for j in range(CONV_W))
    return y + b


RW_L = 64
RW_SUB = 16
RW_PAIRS = RW_HEADS // 2


def _bdot(a, b):
    return jnp.dot(a.astype(jnp.bfloat16), b.astype(jnp.bfloat16), preferred_element_type=jnp.float32)


def _bdot_nt(a, b):
    return lax.dot_general(a.astype(jnp.bfloat16), b.astype(jnp.bfloat16), (((1,), (1,)), ((), ())),
                           preferred_element_type=jnp.float32)


def _rwkv_chunk(lws, rs, ks, vs, kks, as_, states, reverse):
    L, L2 = RW_L, 2 * RW_L
    P = range(len(lws))
    row = lax.broadcasted_iota(jnp.int32, (L, L), 0)
    col = lax.broadcasted_iota(jnp.int32, (L, L), 1)
    before = (lambda a, b: a > b) if reverse else (lambda a, b: a < b)
    tri = jnp.logical_not(before(row, col)).astype(jnp.float32)
    row2 = lax.broadcasted_iota(jnp.int32, (L2, L2), 0)
    col2 = lax.broadcasted_iota(jnp.int32, (L2, L2), 1)
    same = (row2 // L) == (col2 // L)
    strict = same & before(col2, row2)
    incl = same & jnp.logical_not(before(row2, col2))
    diag_blk = (row2 // RW_SUB) == (col2 // RW_SUB)
    eye = (row2 == col2).astype(jnp.float32)
    head_a = (lax.broadcasted_iota(jnp.int32, (1, LANES), 1) // RW_HD) == 0

    def stack2(x):
        return jnp.concatenate([jnp.where(head_a, x, 0.0), jnp.where(head_a, 0.0, x)], axis=0)

    cum = [jnp.dot(tri, lws[p], precision=lax.Precision.HIGHEST, preferred_element_type=jnp.float32) for p in P]
    last = 0 if reverse else L - 1
    cum_l = [cum[p][last:last + 1, :] for p in P]
    b = [as_[p] * kks[p] for p in P]
    ginv = [jnp.exp(-cum[p]) for p in P]
    gtail = [jnp.exp(cum_l[p] - cum[p]) for p in P]
    left = [jnp.concatenate([stack2(kks[p] * jnp.exp(cum[p] - lws[p])), stack2(rs[p] * jnp.exp(cum[p]))], axis=0)
            for p in P]
    right = [jnp.concatenate([stack2(b[p] * ginv[p]), stack2(ks[p] * ginv[p])], axis=0) for p in P]
    v2 = [stack2(vs[p]) for p in P]
    g = [_bdot_nt(left[p], right[p]) for p in P]
    wy = [_bdot_nt(left[p], states[p]) for p in P]
    n = [jnp.where(strict, g[p][:L2, :L2], 0.0) for p in P]
    mkk = [jnp.where(strict, g[p][:L2, L2:], 0.0) for p in P]
    mrb = [jnp.where(incl, g[p][L2:, :L2], 0.0) for p in P]
    mrk = [jnp.where(incl, g[p][L2:, L2:], 0.0) for p in P]
    d = [jnp.where(diag_blk, n[p], 0.0) for p in P]
    d2 = [_bdot(d[p], d[p]) for p in P]
    d4 = [_bdot(d2[p], d2[p]) for p in P]
    d8 = [_bdot(d4[p], d4[p]) for p in P]
    tinv = [_bdot(eye - d[p], eye + d2[p]) for p in P]
    tinv = [_bdot(tinv[p], eye + d4[p]) for p in P]
    tinv = [_bdot(tinv[p], eye + d8[p]) for p in P]
    rhs = [wy[p][:L2] + _bdot(mkk[p], v2[p]) for p in P]

    nblk = L // RW_SUB
    zero_blk = jnp.zeros((RW_SUB, LANES), jnp.float32)

    def pick(x, blk):
        lo = blk * RW_SUB
        return jnp.concatenate([x[lo:lo + RW_SUB], x[L + lo:L + lo + RW_SUB]], axis=0)

    def place(blocks):
        half = lambda h: [zero_blk if blocks[j] is None else blocks[j][h * RW_SUB:(h + 1) * RW_SUB]
                          for j in range(nblk)]
        return jnp.concatenate(half(0) + half(1), axis=0)

    u_blocks = [[None] * nblk for _ in P]
    order = list(reversed(range(nblk))) if reverse else list(range(nblk))
    for blk in order:
        for p in P:
            acc = pick(rhs[p], blk)
            if blk != order[0]:
                acc = acc - _bdot(pick(n[p], blk), place(u_blocks[p]))
            cur = [None] * nblk
            cur[blk] = acc
            u_blocks[p][blk] = _bdot(pick(tinv[p], blk), place(cur))
    u = [place(u_blocks[p]) for p in P]
    y2 = [wy[p][L2:] - _bdot(mrb[p], u[p]) + _bdot(mrk[p], v2[p]) for p in P]
    ys = [y2[p][:L] + y2[p][L:] for p in P]
    upd = [_bdot(v2[p].T, stack2(ks[p] * gtail[p])) - _bdot(u[p].T, stack2(b[p] * gtail[p])) for p in P]
    s_news = [states[p] * jnp.exp(cum_l[p]) + upd[p] for p in P]
    return ys, s_news


def _rwkv_kernel(lw_ref, r_ref, k_ref, v_ref, kk_ref, a_ref, s0_ref, y_ref, sT_ref, s_scr, *, reverse):
    c = pl.program_id(1)

    @pl.when(c == 0)
    def _():
        s_scr[...] = s0_ref[0]

    slabs = [pl.ds(p * LANES, LANES) for p in range(RW_PAIRS)]
    ys, s_news = _rwkv_chunk(*[[ref[0, :, sl] for sl in slabs]
                               for ref in (lw_ref, r_ref, k_ref, v_ref, kk_ref, a_ref)],
                             [s_scr[p] for p in range(RW_PAIRS)], reverse)
    for p in range(RW_PAIRS):
        y_ref[0, :, slabs[p]] = ys[p]
        s_scr[p] = s_news[p]

    @pl.when(c == pl.num_programs(1) - 1)
    def _():
        sT_ref[0] = s_scr[...]


def rwkv7_scan(rkv, lw, k, kk, a, S0, reverse=False):
    B, T, C = lw.shape
    assert C == RW_PAIRS * LANES and T % RW_L == 0 and rkv.shape[-1] >= 3 * C
    nc = T // RW_L
    chunk = (lambda c: nc - 1 - c) if reverse else (lambda c: c)
    seq = pl.BlockSpec((1, RW_L, C), lambda b, c: (b, chunk(c), 0))
    r_spec = seq
    v_spec = pl.BlockSpec((1, RW_L, C), lambda b, c: (b, chunk(c), 2))
    st = pl.BlockSpec((1, RW_PAIRS, LANES, LANES), lambda b, c: (b, 0, 0, 0))
    y, sT = pl.pallas_call(
        functools.partial(_rwkv_kernel, reverse=reverse),
        grid=(B, nc),
        in_specs=[seq, r_spec, seq, v_spec, seq, seq, st],
        out_specs=[seq, st],
        out_shape=[jax.ShapeDtypeStruct((B, T, C), jnp.float32),
                   jax.ShapeDtypeStruct((B, RW_PAIRS, LANES, LANES), jnp.float32)],
        scratch_shapes=[pltpu.VMEM((RW_PAIRS, LANES, LANES), jnp.float32)],
        compiler_params=pltpu.CompilerParams(dimension_semantics=("arbitrary", "arbitrary"),
                                             vmem_limit_bytes=VMEM_LIMIT_BYTES),
        name="rwkv7_scan",
    )(lw, rkv, k, rkv, kk, a, S0)
    return y, sT


def rwkv_stream(p, S0, w0, w_up, a0, a_up, g_up, k_k, k_a, r_k, ln_g, ln_b):
    B, T, _ = p.shape
    r, k, v, wd, ad, gd = split_cols(p, [RW_WIDTH, RW_WIDTH, RW_WIDTH, RW_DECAY_RANK, RW_A_RANK])
    heads = lambda t: t.reshape(B, T, RW_HEADS, RW_HD)
    kk = heads(k * k_k).astype(jnp.float32)
    kk = (kk * lax.rsqrt(jnp.maximum(jnp.sum(kk * kk, -1, keepdims=True), 1e-12))).reshape(B, T, RW_WIDTH)
    g = jax.nn.sigmoid(gd) @ g_up
    per_dir, bonus = [], []
    for d in range(2):
        wl = (w0[d] + jnp.tanh(wd) @ w_up[d]).astype(jnp.float32)
        lw = -jnp.exp(-jax.nn.softplus(-wl) - 0.5)
        a = jax.nn.sigmoid(a0[d] + ad @ a_up[d])
        kd = k * (1 + (a - 1) * k_a)
        per_dir.append(rwkv7_scan(p, lw, kd, kk, a, S0[d], reverse=(d == 1)))
        bonus.append(jnp.sum(heads(r) * heads(kd) * r_k, -1, keepdims=True) * heads(v))
    sT = (per_dir[0][1], per_dir[1][1])
    y = (per_dir[0][0] + per_dir[1][0]).astype(jnp.float32).reshape(B, T, RW_HEADS, RW_HD)
    mu = jnp.mean(y, -1, keepdims=True)
    var = jnp.mean(jnp.square(y - mu), -1, keepdims=True)
    yn = ((y - mu) * lax.rsqrt(var + RW_GN_EPS)).reshape(B, T, RW_WIDTH) * ln_g + ln_b
    out = (yn + (bonus[0] + bonus[1]).reshape(B, T, RW_WIDTH)) * g
    return out.astype(p.dtype), sT


GLA_SUB = 16
GLA_PAIRS = GLA_HEADS // 2
assert GLA_DV == LANES and 2 * GLA_DK == LANES


def _gla_kernel(q_ref, k_ref, g_ref, v_ref, s0_ref, o_ref, sT_ref, s_scr, *, reverse):
    c = pl.program_id(1)
    L = GLA_CHUNK

    @pl.when(c == 0)
    def _():
        s_scr[...] = s0_ref[0]

    row = lax.broadcasted_iota(jnp.int32, (L, L), 0)
    col = lax.broadcasted_iota(jnp.int32, (L, L), 1)
    incl = (col >= row) if reverse else (col <= row)
    last = 0 if reverse else L - 1
    lane = lax.broadcasted_iota(jnp.int32, (1, LANES), 1)
    trow = lax.broadcasted_iota(jnp.int32, (L, LANES), 0)
    outs = []
    for p in range(GLA_PAIRS):
        sl = pl.ds(p * LANES, LANES)
        q, k, g = q_ref[0, :, sl], k_ref[0, :, sl], g_ref[0, :, sl]
        b = jnp.dot(incl.astype(jnp.float32), g, precision=lax.Precision.HIGHEST,
                    preferred_element_type=jnp.float32)
        b_l = b[last:last + 1, :]
        qe = q * jnp.exp(b)
        kd = k * jnp.exp(b_l - b)
        qfs, kfs = [], []
        for blk in range(L // GLA_SUB):
            lo, hi = blk * GLA_SUB, (blk + 1) * GLA_SUB
            first = hi - 1 if reverse else lo
            b_ref = b[first:first + 1, :]
            qfs.append(q[lo:hi, :] * jnp.exp(b[lo:hi, :] - b_ref))
            used = (trow >= lo) if reverse else (trow < hi)
            kfs.append(jnp.where(used, k * jnp.exp(jnp.where(used, b_ref - b, 0.0)), 0.0))
        for hh in range(2):
            h = 2 * p + hh
            m = (lane // GLA_DK) == hh
            v = v_ref[0, :, pl.ds(h * GLA_DV, GLA_DV)]
            s_t = s_scr[h]
            a = jnp.concatenate([_bdot_nt(qf, jnp.where(m, kf, 0.0)) for qf, kf in zip(qfs, kfs)], axis=0)
            o = _bdot(jnp.where(incl, a, 0.0), v) + _bdot_nt(qe, s_t)
            s_new = s_t * jnp.exp(b_l) + _bdot(v.T, jnp.where(m, kd, 0.0))
            outs.append((h, o, s_new))
    for h, o, s_new in outs:
        o_ref[0, :, pl.ds(h * GLA_DV, GLA_DV)] = o
        s_scr[h] = s_new

    @pl.when(c == pl.num_programs(1) - 1)
    def _():
        sT_ref[0] = s_scr[...]


def gla_chunked(q, k, g, v, S0, reverse=False):
    B, T, _ = q.shape
    assert T % GLA_CHUNK == 0
    nc = T // GLA_CHUNK
    chunk = (lambda b, c: (b, nc - 1 - c, 0)) if reverse else (lambda b, c: (b, c, 0))
    qk = pl.BlockSpec((1, GLA_CHUNK, GLA_HEADS * GLA_DK), chunk)
    vv = pl.BlockSpec((1, GLA_CHUNK, GLA_HEADS * GLA_DV), chunk)
    st = pl.BlockSpec((1, GLA_HEADS, LANES, LANES), lambda b, c: (b, 0, 0, 0))
    return pl.pallas_call(
        functools.partial(_gla_kernel, reverse=reverse),
        grid=(B, nc),
        in_specs=[qk, qk, qk, vv, st],
        out_specs=[vv, st],
        out_shape=[jax.ShapeDtypeStruct((B, T, GLA_HEADS * GLA_DV), jnp.float32),
                   jax.ShapeDtypeStruct((B, GLA_HEADS, LANES, LANES), jnp.float32)],
        scratch_shapes=[pltpu.VMEM((GLA_HEADS, LANES, LANES), jnp.float32)],
        compiler_params=pltpu.CompilerParams(dimension_semantics=("arbitrary", "arbitrary"),
                                             vmem_limit_bytes=VMEM_LIMIT_BYTES),
        name="gla_chunked",
    )(q, k, g, v, S0)


def gla_stream(p, S0, gate_up, gate_b, norm_g):
    B, T, _ = p.shape
    q, k, v, gd, og = split_cols(p, [GLA_HEADS * GLA_DK, GLA_HEADS * GLA_DK, GLA_HEADS * GLA_DV, GLA_GATE_RANK])
    q = q * GLA_DK ** -0.5
    outs, states = [], []
    for d in range(2):
        glog = jax.nn.log_sigmoid((gd @ gate_up[d] + gate_b[d]).astype(jnp.float32)) / GLA_GATE_NORM
        od, sd = gla_chunked(q, k, glog, v, S0[d], reverse=(d == 1))
        outs.append(od)
        states.append(sd)
    o = (outs[0] + outs[1]).reshape(B, T, GLA_HEADS, GLA_DV)
    o = rmsnorm(o, norm_g.reshape(GLA_HEADS, GLA_DV)).reshape(B, T, GLA_HEADS * GLA_DV)
    return (o * jax.nn.silu(og)).astype(p.dtype), tuple(states)


def ssd_chunked(x, dt, A, Bm, Cm, S0):
    Bsz, T, H, P = x.shape
    N = Bm.shape[-1]
    L = SSD_CHUNK
    n = T // L
    f = lambda t: t.astype(jnp.float32)
    xc = (f(x) * f(dt)[..., None]).reshape(Bsz, n, L, H, P)
    Bc = f(Bm).reshape(Bsz, n, L, H, N)
    Cc = f(Cm).reshape(Bsz, n, L, H, N)
    acs = jnp.cumsum((f(dt) * A).reshape(Bsz, n, L, H).transpose(0, 3, 1, 2), -1)
    mask = jnp.tril(jnp.ones((L, L), bool))
    Lmat = jnp.exp(jnp.where(mask, acs[..., :, None] - acs[..., None, :], -jnp.inf))
    G = jnp.einsum("bclhn,bcshn->bhcls", Cc, Bc) * Lmat
    y_diag = jnp.einsum("bhcls,bcshp->bclhp", G, xc)
    contrib = jnp.einsum("bclhn,bhcl,bclhp->bchpn", Bc, jnp.exp(acs[..., -1:] - acs), xc)
    chunk_decay = jnp.exp(acs[..., -1])

    def step(S, inp):
        dec, st = inp
        return dec[..., None, None] * S + st, S

    S, S_in = lax.scan(step, S0, (jnp.moveaxis(chunk_decay, 2, 0), jnp.moveaxis(contrib, 1, 0)))
    y_off = jnp.einsum("bclhn,bchpn,bhcl->bclhp", Cc, jnp.moveaxis(S_in, 0, 1), jnp.exp(acs))
    return (y_diag + y_off).reshape(Bsz, T, H, P).astype(x.dtype), S


def ssd_stream(p, S0f, S0b, conv_w, conv_b, dt_bias, A_log, D_skip, norm_g):
    B, T, _ = p.shape
    z, xbc, dtr = split_cols(p, [SSD_INNER, SSD_CONV_CH])
    xbc = jax.nn.silu(dwconv(xbc, conv_w, conv_b))
    xs, Bm, Cm = split_cols(xbc, [SSD_INNER, SSD_GROUPS * SSD_N])
    xs = xs.reshape(B, T, SSD_HEADS, SSD_P)
    rep = SSD_HEADS // SSD_GROUPS
    Bm = jnp.repeat(Bm.reshape(B, T, SSD_GROUPS, SSD_N), rep, axis=2)
    Cm = jnp.repeat(Cm.reshape(B, T, SSD_GROUPS, SSD_N), rep, axis=2)
    ys, states = [], []
    for d, S0 in enumerate((S0f, S0b)):
        dt = jax.nn.softplus((dtr + dt_bias[d]).astype(jnp.float32))
        A = -jnp.exp(A_log[d].astype(jnp.float32))
        args = [xs, dt, Bm, Cm]
        if d == 1:
            args = [flip(t) for t in args]
        yd, Sd = ssd_chunked(args[0], args[1], A, args[2], args[3], S0)
        ys.append(flip(yd) if d == 1 else yd)
        states.append(Sd)
    y = (ys[0] + ys[1] + xs * D_skip[:, None]).reshape(B, T, SSD_INNER) * jax.nn.silu(z)
    return rmsnorm(y, norm_g).astype(p.dtype), states[0], states[1]


def linear_scan(a, b, h0):
    b = b.at[:, 0].add(a[:, 0] * h0)
    comb = lambda l, r: (l[0] * r[0], r[0] * l[1] + r[1])
    _, h = lax.associative_scan(comb, (a, b), axis=1)
    return h, h[:, -1]


def lru_stream(p, h0f, h0b, conv_w, conv_b, wa, ba, wx, bx, lam):
    B, T, _ = p.shape
    xr, gate = split_cols(p, [LRU_WIDTH])
    xf = dwconv(xr, conv_w, conv_b).astype(jnp.float32)
    xh = xf.reshape(B, T, LRU_BLOCKS, LRU_BD)
    hs, finals = [], []
    for d, h0 in enumerate((h0f, h0b)):
        r = jax.nn.sigmoid(jnp.einsum("btgi,gij->btgj", xh, wa[d]).reshape(B, T, LRU_WIDTH) + ba[d])
        i = jax.nn.sigmoid(jnp.einsum("btgi,gij->btgj", xh, wx[d]).reshape(B, T, LRU_WIDTH) + bx[d])
        log_a = -LRU_C * r * jax.nn.softplus(-lam[d])
        a = jnp.exp(log_a)
        b = jnp.sqrt(-jnp.expm1(2 * log_a)) * (i * xf)
        if d == 1:
            a, b = flip(a), flip(b)
        h, hT = linear_scan(a, b, h0)
        hs.append(flip(h) if d == 1 else h)
        finals.append(hT)
    y = (hs[0] + hs[1]).astype(p.dtype) * jax.nn.gelu(gate)
    return y, finals[0], finals[1]


def even_mixer(u, uc, w_in, mu, rw_p, gla_p):
    B = u.shape[0]
    p, pc = mm3(u, w_in), mm3(uc, w_in)
    rw, gl = p[..., :RW_COLS], p[..., RW_COLS:]
    rwc, glc = pc[..., :RW_COLS], pc[..., RW_COLS:]
    rw = rw + mu * (qshift_grid(rw) - rw)
    rwc = rwc + mu * (shift_seq(rwc) - rwc)
    z_rw = jnp.zeros((B, RW_PAIRS, LANES, LANES), jnp.float32)
    yc_rw, s_ctx = rwkv_stream(rwc, (z_rw, z_rw), *rw_p)
    y_rw, _ = rwkv_stream(rw, s_ctx, *rw_p)
    z_gla = jnp.zeros((B, GLA_HEADS, LANES, LANES), jnp.float32)
    yc_gl, g_ctx = gla_stream(glc, (z_gla, z_gla), *gla_p)
    y_gl, _ = gla_stream(gl, g_ctx, *gla_p)
    return (y_rw, y_gl), (yc_rw, yc_gl)


def odd_mixer(u, uc, w_in, ssd_p, lru_p):
    B = u.shape[0]
    p = grid_to_cols(mm3(u, w_in))
    pc = mm3(uc, w_in)
    z_s = jnp.zeros((B, SSD_HEADS, SSD_P, SSD_N), jnp.float32)
    yc_s, Sf, Sb = ssd_stream(pc[..., :SSD_COLS], z_s, z_s, *ssd_p)
    y_s, _, _ = ssd_stream(p[..., :SSD_COLS], Sf, Sb, *ssd_p)
    z_l = jnp.zeros((B, LRU_WIDTH), jnp.float32)
    yc_l, hf, hb = lru_stream(pc[..., SSD_COLS:], z_l, z_l, *lru_p)
    y_l, _, _ = lru_stream(p[..., SSD_COLS:], hf, hb, *lru_p)
    return (cols_to_grid(y_s), cols_to_grid(y_l)), (yc_s, yc_l)


def peer(x, wq, keys, u_tab, v_tab):
    B, T, D = x.shape
    M = B * T
    xf = x.reshape(M, D)
    idx, gate_t = peer_route(xf, wq, keys)
    return peer_experts(xf, idx, gate_t, u_tab, v_tab).reshape(B, T, D)


def kernel(x, c, ctx, c_ctx, ada_w, ada_b, norm1_g, norm2_g, ev_w_in, rw_mu, rw_w0, rw_w_up, rw_a0, rw_a_up,
           rw_g_up, rw_k_k, rw_k_a, rw_r_k, rw_ln_g, rw_ln_b, gla_gate_up, gla_gate_b, gla_norm_g, od_w_in,
           ssd_conv_w, ssd_conv_b, ssd_dt_bias, ssd_A_log, ssd_D, ssd_norm_g, lru_conv_w, lru_conv_b, lru_wa,
           lru_ba, lru_wx, lru_bx, lru_lam, w_out, peer_wq, peer_keys, peer_u, peer_v, final_g):
    h, hc = x, ctx
    sc, scc = jax.nn.silu(c), jax.nn.silu(c_ctx)
    for l in range(DEPTH):
        last = l == DEPTH - 1
        i = l // 2
        mod = jnp.split((sc @ ada_w[l] + ada_b[l])[:, None, :], 6, axis=-1)
        modc = jnp.split(scc @ ada_w[l] + ada_b[l], 6, axis=-1)
        u = modulate(h, norm1_g[l], mod[0], mod[1])
        uc = modulate(hc, norm1_g[l], modc[0], modc[1])
        if l % 2 == 0:
            rw_p = (rw_w0[i], rw_w_up[i], rw_a0[i], rw_a_up[i], rw_g_up[i], rw_k_k[i], rw_k_a[i], rw_r_k[i],
                    rw_ln_g[i], rw_ln_b[i])
            gla_p = (gla_gate_up[i], gla_gate_b[i], gla_norm_g[i])
            y, yc = even_mixer(u, uc, ev_w_in[i], rw_mu[i], rw_p, gla_p)
        else:
            ssd_p = (ssd_conv_w[i], ssd_conv_b[i], ssd_dt_bias[i], ssd_A_log[i], ssd_D[i], ssd_norm_g[i])
            lru_p = (lru_conv_w[i], lru_conv_b[i], lru_wa[i], lru_ba[i], lru_wx[i], lru_bx[i], lru_lam[i])
            y, yc = odd_mixer(u, uc, od_w_in[i], ssd_p, lru_p)
        h = h + mod[2] * mm3(y, w_out[l])
        u = modulate(h, norm2_g[l], mod[3], mod[4])
        h = h + mod[5] * peer(u, peer_wq[l], peer_keys[l], peer_u[l], peer_v[l])
        if not last:
            hc = hc + modc[2] * mm3(yc, w_out[l])
            uc = modulate(hc, norm2_g[l], modc[3], modc[4])
            hc = hc + modc[5] * peer(uc, peer_wq[l], peer_keys[l], peer_u[l], peer_v[l])
    return rmsnorm(h, final_g)
```

```python
import functools

import jax
import jax.numpy as jnp
from jax import lax
import numpy as np
from jax.experimental import pallas as pl
from jax.experimental.pallas import tpu as pltpu

D_MODEL = 1024
DEPTH = 4
GRID_W = 64
MIX_WIDTH = D_MODEL

RW_WIDTH = MIX_WIDTH // 2
RW_HD = 64
RW_HEADS = RW_WIDTH // RW_HD
RW_DECAY_RANK = 32
RW_A_RANK = 32
RW_GATE_RANK = 64
RW_GN_EPS = 64e-5
RW_COLS = 3 * RW_WIDTH + RW_DECAY_RANK + RW_A_RANK + RW_GATE_RANK
GLA_HEADS = 4
GLA_DV = (MIX_WIDTH - RW_WIDTH) // GLA_HEADS
GLA_DK = GLA_DV // 2
GLA_GATE_RANK = 16
GLA_GATE_NORM = 16.0
GLA_CHUNK = 64
SSD_INNER = MIX_WIDTH // 2
SSD_P = 64
SSD_HEADS = SSD_INNER // SSD_P
SSD_N = 128
SSD_GROUPS = 2
SSD_CHUNK = 128
SSD_CONV_CH = SSD_INNER + 2 * SSD_GROUPS * SSD_N
SSD_COLS = SSD_INNER + SSD_CONV_CH + SSD_HEADS
LRU_WIDTH = MIX_WIDTH - SSD_INNER
LRU_BLOCKS = 8
LRU_BD = LRU_WIDTH // LRU_BLOCKS
LRU_C = 8.0
CONV_W = 4
PEER_HEADS = 8
N_KEYS = 128
PEER_TOPK = 16
PEER_DQ = 256
PEER_BLOCK = 128

VMEM_LIMIT_BYTES = 56 * 1024 * 1024


def _mm_kernel(a_ref, b_ref, o_ref):
    o_ref[...] = jnp.dot(a_ref[...].astype(jnp.bfloat16), b_ref[...], preferred_element_type=jnp.float32)


def pmm(a, b, tm=256):
    M, K = a.shape
    N = b.shape[1]
    assert M % tm == 0
    return pl.pallas_call(
        _mm_kernel,
        grid=(M // tm,),
        in_specs=[pl.BlockSpec((tm, K), lambda i: (i, 0)), pl.BlockSpec((K, N), lambda i: (0, 0))],
        out_specs=pl.BlockSpec((tm, N), lambda i: (i, 0)),
        out_shape=jax.ShapeDtypeStruct((M, N), jnp.float32),
        compiler_params=pltpu.CompilerParams(dimension_semantics=("arbitrary",),
                                             vmem_limit_bytes=VMEM_LIMIT_BYTES),
        name="pmm",
    )(a, b.astype(jnp.bfloat16))


SUBLANES = 8
LANES = 128
PEER_E = PEER_HEADS * PEER_TOPK
PEER_TB = 8
PEER_SLOTS = 3


def _fold8(p):
    sub = lax.broadcasted_iota(jnp.int32, (SUBLANES, LANES), 0)

    def comb(a, b, k):
        fa = a + pltpu.roll(a, SUBLANES - k, 0)
        fb = b + pltpu.roll(b, k, 0)
        return jnp.where((sub & k) == 0, fa, fb)

    return comb(comb(comb(p[0], p[4], 4), comb(p[2], p[6], 4), 2),
                comb(comb(p[1], p[5], 4), comb(p[3], p[7], 4), 2), 1)


def _peer_kernel(idx_cur_ref, idx_nx1_ref, idx_nxt_ref, gate_ref, x_ref, tab_ref, o_ref, buf, sem):
    i = pl.program_id(0)
    n = pl.num_programs(0)
    slot = i % PEER_SLOTS
    nslot = (i + PEER_SLOTS - 1) % PEER_SLOTS

    def issue(idx_ref, t, s, e0=0, e1=PEER_E):
        for e in range(e0, e1):
            j = t * PEER_E + e
            pltpu.make_async_copy(tab_ref.at[idx_ref[0, 0, j]], buf.at[s, j], sem.at[s]).start(priority=e % 2)

    def wait_slot(s):
        pltpu.make_async_copy(buf.at[s], buf.at[s], sem.at[s]).wait()

    @pl.when(i == 0)
    def _():
        lax.fori_loop(0, PEER_TB, lambda t, c: (issue(idx_cur_ref, t, 0), c)[1], 0)
        lax.fori_loop(0, PEER_TB, lambda t, c: (issue(idx_nx1_ref, t, 1), c)[1], 0)

    wait_slot(slot)

    n_groups = PEER_E // SUBLANES
    per_group = PEER_E // (2 * n_groups)
    n_acc = 4

    def token(t, carry):
        xrow = x_ref[pl.ds(t, 1), :]
        xt = jnp.concatenate([xrow[:, s * LANES:(s + 1) * LANES] for s in range(SUBLANES)], axis=0)
        base = t * PEER_E
        tok = (i % (ROUTE_TM // PEER_TB)) * PEER_TB + t
        onehot = (lax.broadcasted_iota(jnp.int32, (SUBLANES, ROUTE_TM), 1) == tok).astype(jnp.float32)
        coefs = []
        for g in range(n_groups):
            rows = [buf[slot, base + g * SUBLANES + j, 0] for j in range(SUBLANES)]
            issue(idx_nxt_ref, t, nslot, g * per_group, (g + 1) * per_group)
            a = jnp.sum(_fold8([r * xt for r in rows]), axis=1, keepdims=True)
            gate = jnp.sum(gate_ref[0, pl.ds(g * SUBLANES, SUBLANES), :] * onehot, axis=1, keepdims=True)
            coefs.append(jnp.broadcast_to(jax.nn.gelu(a) * gate, (SUBLANES, LANES)))
        accs = [jnp.zeros((SUBLANES, LANES), jnp.float32) for _ in range(n_acc)]
        for g in range(n_groups):
            rows = [buf[slot, base + g * SUBLANES + j, 1] for j in range(SUBLANES)]
            issue(idx_nxt_ref, t, nslot, (n_groups + g) * per_group, (n_groups + g + 1) * per_group)
            for j in range(SUBLANES):
                accs[j % n_acc] = accs[j % n_acc] + coefs[g][j:j + 1, :] * rows[j]
        acc = (accs[0] + accs[1]) + (accs[2] + accs[3])
        o_ref[pl.ds(t, 1), :] = jnp.concatenate([acc[s:s + 1, :] for s in range(SUBLANES)], axis=1)
        return carry

    lax.fori_loop(0, PEER_TB, token, 0)

    @pl.when(i == n - 1)
    def _():
        wait_slot((i + 1) % PEER_SLOTS)
        wait_slot(nslot)


def peer_experts(xf, idx, gate_t, u_tab, v_tab):
    M, D = xf.shape
    NE = u_tab.shape[0]
    assert D == SUBLANES * LANES and M % ROUTE_TM == 0 and ROUTE_TM % PEER_TB == 0 and idx.shape == (M, PEER_E)
    assert gate_t.shape == (M // ROUTE_TM, PEER_E, ROUTE_TM)
    nblk = M // PEER_TB
    tab = jnp.stack([u_tab.reshape(NE, SUBLANES, LANES), v_tab.reshape(NE, SUBLANES, LANES)], axis=1)
    idx3 = idx.astype(jnp.int32).reshape(nblk, 1, PEER_TB * PEER_E)
    smem_blk = (1, 1, PEER_TB * PEER_E)
    out = pl.pallas_call(
        _peer_kernel,
        grid=(nblk,),
        in_specs=[
            pl.BlockSpec(smem_blk, lambda i: (i, 0, 0), memory_space=pltpu.SMEM),
            pl.BlockSpec(smem_blk, lambda i: (jnp.minimum(i + 1, nblk - 1), 0, 0), memory_space=pltpu.SMEM),
            pl.BlockSpec(smem_blk, lambda i: (jnp.minimum(i + 2, nblk - 1), 0, 0), memory_space=pltpu.SMEM),
            pl.BlockSpec((1, PEER_E, ROUTE_TM), lambda i: (i // (ROUTE_TM // PEER_TB), 0, 0)),
            pl.BlockSpec((PEER_TB, D), lambda i: (i, 0)),
            pl.BlockSpec(memory_space=pl.ANY),
        ],
        out_specs=pl.BlockSpec((PEER_TB, D), lambda i: (i, 0)),
        out_shape=jax.ShapeDtypeStruct((M, D), jnp.float32),
        scratch_shapes=[
            pltpu.VMEM((PEER_SLOTS, PEER_TB * PEER_E, 2, SUBLANES, LANES), jnp.float32),
            pltpu.SemaphoreType.DMA((PEER_SLOTS,)),
        ],
        compiler_params=pltpu.CompilerParams(dimension_semantics=("arbitrary",),
                                             vmem_limit_bytes=VMEM_LIMIT_BYTES),
        name="peer_experts",
    )(idx3, idx3, idx3, gate_t.astype(jnp.float32), xf, tab)
    return out


ROUTE_TM = 128
PEER_HALF = PEER_DQ // 2


def _topk_rows(s, k, rank=None, payload=None):
    if rank is None:
        rank = lax.broadcasted_iota(jnp.int32, s.shape, 0)
    big = jnp.iinfo(jnp.int32).max
    vals, sel = [], []
    for _ in range(k):
        m = jnp.max(s, axis=0, keepdims=True)
        pos = jnp.min(jnp.where(s == m, rank, big), axis=0, keepdims=True)
        hit = rank == pos
        vals.append(m)
        sel.append(pos if payload is None else jnp.max(jnp.where(hit, payload, -1), axis=0, keepdims=True))
        s = jnp.where(hit, -jnp.inf, s)
    return jnp.concatenate(vals, axis=0), jnp.concatenate(sel, axis=0)


def _product_candidates(top_s, top_i):
    K = PEER_TOPK
    s1, s2 = top_s
    i1, i2 = top_i
    TM = s1.shape[1]
    vals, flat, eid = [], [], []
    for i in range(2):
        nj = K // (i + 1)
        j = lax.broadcasted_iota(jnp.int32, (nj, TM), 0)
        vals.append(s1[i:i + 1, :] + s2[:nj, :])
        flat.append(i * K + j)
        eid.append(i1[i:i + 1, :] * N_KEYS + i2[:nj, :])
    for j in range(K // 3):
        ni_end = K // (j + 1)
        rows = K if ni_end > SUBLANES else SUBLANES
        i = lax.broadcasted_iota(jnp.int32, (rows, TM), 0)
        ok = (i >= 2) & (i < ni_end)
        vals.append(jnp.where(ok, s1[:rows, :] + s2[j:j + 1, :], -jnp.inf))
        flat.append(i * K + j)
        eid.append(i1[:rows, :] * N_KEYS + i2[j:j + 1, :])
    cat = lambda xs: jnp.concatenate(xs, axis=0)
    return cat(vals), cat(flat), cat(eid)


def _route_kernel(x_ref, wq_ref, keys_ref, idx_ref, gate_ref):
    q = jnp.dot(x_ref[...].astype(jnp.bfloat16), wq_ref[...], preferred_element_type=jnp.float32)
    for h in range(PEER_HEADS):
        top_s, top_i = [], []
        for half in range(2):
            c0 = (h * 2 + half) * PEER_HALF
            qh = q[:, c0:c0 + PEER_HALF].astype(jnp.bfloat16)
            s = lax.dot_general(keys_ref[h, half], qh, (((1,), (1,)), ((), ())),
                                preferred_element_type=jnp.float32)
            ts, ti = _topk_rows(s, PEER_TOPK)
            top_s.append(ts)
            top_i.append(ti)
        cand_s, cand_flat, cand_i = _product_candidates(top_s, top_i)
        best_s, best_i = _topk_rows(cand_s, PEER_TOPK, cand_flat, cand_i)
        e = jnp.exp(best_s - best_s[0:1, :])
        r0 = h * PEER_TOPK
        gate_ref[0, r0:r0 + PEER_TOPK, :] = e / jnp.sum(e, axis=0, keepdims=True)
        idx_ref[0, r0:r0 + PEER_TOPK, :] = best_i


def peer_route(xf, wq, keys):
    M, D = xf.shape
    assert M % ROUTE_TM == 0
    nblk = M // ROUTE_TM
    out_blk = pl.BlockSpec((1, PEER_E, ROUTE_TM), lambda i: (i, 0, 0))
    idx, gate = pl.pallas_call(
        _route_kernel,
        grid=(nblk,),
        in_specs=[pl.BlockSpec((ROUTE_TM, D), lambda i: (i, 0)),
                  pl.BlockSpec(wq.shape, lambda i: (0, 0)),
                  pl.BlockSpec(keys.shape, lambda i: (0, 0, 0, 0))],
        out_specs=[out_blk, out_blk],
        out_shape=[jax.ShapeDtypeStruct((nblk, PEER_E, ROUTE_TM), jnp.int32),
                   jax.ShapeDtypeStruct((nblk, PEER_E, ROUTE_TM), jnp.float32)],
        compiler_params=pltpu.CompilerParams(dimension_semantics=("arbitrary",),
                                             vmem_limit_bytes=VMEM_LIMIT_BYTES),
        name="peer_route",
    )(xf, wq.astype(jnp.bfloat16), keys.astype(jnp.bfloat16))
    return idx.transpose(0, 2, 1).reshape(M, PEER_E), gate


def _mm_parts_kernel(*refs):
    n = (len(refs) - 1) // 2
    acc = None
    for a_ref, b_ref in zip(refs[:n], refs[n:2 * n]):
        part = jnp.dot(a_ref[...].astype(jnp.bfloat16), b_ref[...], preferred_element_type=jnp.float32)
        acc = part if acc is None else acc + part
    refs[-1][...] = acc


def pmm_parts(parts, w, tm=256):
    M = parts[0].shape[0]
    N = w.shape[1]
    assert M % tm == 0 and sum(p.shape[1] for p in parts) == w.shape[0]
    offs = np.cumsum([0] + [p.shape[1] for p in parts])
    ws = [w[int(offs[j]):int(offs[j + 1])].astype(jnp.bfloat16) for j in range(len(parts))]
    return pl.pallas_call(
        _mm_parts_kernel,
        grid=(M // tm,),
        in_specs=[pl.BlockSpec((tm, p.shape[1]), lambda i: (i, 0)) for p in parts]
                 + [pl.BlockSpec(wj.shape, lambda i: (0, 0)) for wj in ws],
        out_specs=pl.BlockSpec((tm, N), lambda i: (i, 0)),
        out_shape=jax.ShapeDtypeStruct((M, N), jnp.float32),
        compiler_params=pltpu.CompilerParams(dimension_semantics=("arbitrary",),
                                             vmem_limit_bytes=VMEM_LIMIT_BYTES),
        name="pmm_parts",
    )(*parts, *ws)


def mm3(x, w):
    if isinstance(x, tuple):
        B, T, _ = x[0].shape
        return pmm_parts([p.reshape(B * T, p.shape[-1]) for p in x], w).reshape(B, T, w.shape[1])
    B, T, K = x.shape
    return pmm(x.reshape(B * T, K), w).reshape(B, T, w.shape[1])


def rmsnorm(x, g, eps=1e-6):
    xf = x.astype(jnp.float32)
    y = xf * lax.rsqrt(jnp.mean(xf * xf, -1, keepdims=True) + eps)
    return (y * g.astype(jnp.float32)).astype(x.dtype)


def modulate(x, g, shift, scale):
    return rmsnorm(x, g) * (1 + scale) + shift


def split_cols(p, sizes):
    return jnp.split(p, [int(i) for i in np.cumsum(sizes)], axis=-1)


def flip(t):
    return jnp.flip(t, axis=1)


def grid_to_cols(x):
    B, T, C = x.shape
    rows = T // GRID_W
    return x.reshape(B, rows, GRID_W, C).transpose(0, 2, 1, 3).reshape(B, T, C)


def cols_to_grid(x):
    B, T, C = x.shape
    rows = T // GRID_W
    return x.reshape(B, GRID_W, rows, C).transpose(0, 2, 1, 3).reshape(B, T, C)


def qshift_grid(x):
    B, T, C = x.shape
    rows = T // GRID_W
    g = jnp.pad(x.reshape(B, rows, GRID_W, C), ((0, 0), (1, 1), (1, 1), (0, 0)))
    sel = jnp.arange(C) % 4
    out = jnp.where(sel == 0, g[:, 1:-1, :-2], jnp.where(sel == 1, g[:, 1:-1, 2:],
                    jnp.where(sel == 2, g[:, :-2, 1:-1], g[:, 2:, 1:-1])))
    return out.reshape(B, T, C)


def shift_seq(x):
    prev = jnp.pad(x, ((0, 0), (1, 0), (0, 0)))[:, :-1]
    nxt = jnp.pad(x, ((0, 0), (0, 1), (0, 0)))[:, 1:]
    return jnp.where(jnp.arange(x.shape[-1]) % 2 == 0, prev, nxt)


def dwconv(x, w, b):
    lo, hi = (CONV_W - 1) // 2, CONV_W // 2
    T = x.shape[1]
    xp = jnp.pad(x, ((0, 0), (lo, hi), (0, 0)))
    y = sum(xp[:, j:j + T] * w[j].astype(x.dtype) for j in range(CONV_W))
    return y + b


RW_L = 64
RW_SUB = 16
RW_PAIRS = RW_HEADS // 2


def _bdot(a, b):
    return jnp.dot(a.astype(jnp.bfloat16), b.astype(jnp.bfloat16), preferred_element_type=jnp.float32)


def _bdot_nt(a, b):
    return lax.dot_general(a.astype(jnp.bfloat16), b.astype(jnp.bfloat16), (((1,), (1,)), ((), ())),
                           preferred_element_type=jnp.float32)


def _rwkv_chunk(lws, rs, ks, vs, kks, as_, states, reverse):
    L, L2 = RW_L, 2 * RW_L
    P = range(len(lws))
    row = lax.broadcasted_iota(jnp.int32, (L, L), 0)
    col = lax.broadcasted_iota(jnp.int32, (L, L), 1)
    before = (lambda a, b: a > b) if reverse else (lambda a, b: a < b)
    tri = jnp.logical_not(before(row, col)).astype(jnp.float32)
    row2 = lax.broadcasted_iota(jnp.int32, (L2, L2), 0)
    col2 = lax.broadcasted_iota(jnp.int32, (L2, L2), 1)
    same = (row2 // L) == (col2 // L)
    strict = same & before(col2, row2)
    incl = same & jnp.logical_not(before(row2, col2))
    diag_blk = (row2 // RW_SUB) == (col2 // RW_SUB)
    eye = (row2 == col2).astype(jnp.float32)
    head_a = (lax.broadcasted_iota(jnp.int32, (1, LANES), 1) // RW_HD) == 0

    def stack2(x):
        return jnp.concatenate([jnp.where(head_a, x, 0.0), jnp.where(head_a, 0.0, x)], axis=0)

    cum = [jnp.dot(tri, lws[p], precision=lax.Precision.HIGHEST, preferred_element_type=jnp.float32) for p in P]
    last = 0 if reverse else L - 1
    cum_l = [cum[p][last:last + 1, :] for p in P]
    b = [as_[p] * kks[p] for p in P]
    ginv = [jnp.exp(-cum[p]) for p in P]
    gtail = [jnp.exp(cum_l[p] - cum[p]) for p in P]
    left = [jnp.concatenate([stack2(kks[p] * jnp.exp(cum[p] - lws[p])), stack2(rs[p] * jnp.exp(cum[p]))], axis=0)
            for p in P]
    right = [jnp.concatenate([stack2(b[p] * ginv[p]), stack2(ks[p] * ginv[p])], axis=0) for p in P]
    v2 = [stack2(vs[p]) for p in P]
    g = [_bdot_nt(left[p], right[p]) for p in P]
    wy = [_bdot_nt(left[p], states[p]) for p in P]
    n = [jnp.where(strict, g[p][:L2, :L2], 0.0) for p in P]
    mkk = [jnp.where(strict, g[p][:L2, L2:], 0.0) for p in P]
    mrb = [jnp.where(incl, g[p][L2:, :L2], 0.0) for p in P]
    mrk = [jnp.where(incl, g[p][L2:, L2:], 0.0) for p in P]
    d = [jnp.where(diag_blk, n[p], 0.0) for p in P]
    d2 = [_bdot(d[p], d[p]) for p in P]
    d4 = [_bdot(d2[p], d2[p]) for p in P]
    d8 = [_bdot(d4[p], d4[p]) for p in P]
    tinv = [_bdot(eye - d[p], eye + d2[p]) for p in P]
    tinv = [_bdot(tinv[p], eye + d4[p]) for p in P]
    tinv = [_bdot(tinv[p], eye + d8[p]) for p in P]
    rhs = [wy[p][:L2] + _bdot(mkk[p], v2[p]) for p in P]

    nblk = L // RW_SUB
    zero_blk = jnp.zeros((RW_SUB, LANES), jnp.float32)

    def pick(x, blk):
        lo = blk * RW_SUB
        return jnp.concatenate([x[lo:lo + RW_SUB], x[L + lo:L + lo + RW_SUB]], axis=0)

    def place(blocks):
        half = lambda h: [zero_blk if blocks[j] is None else blocks[j][h * RW_SUB:(h + 1) * RW_SUB]
                          for j in range(nblk)]
        return jnp.concatenate(half(0) + half(1), axis=0)

    u_blocks = [[None] * nblk for _ in P]
    order = list(reversed(range(nblk))) if reverse else list(range(nblk))
    for blk in order:
        for p in P:
            acc = pick(rhs[p], blk)
            if blk != order[0]:
                acc = acc - _bdot(pick(n[p], blk), place(u_blocks[p]))
            cur = [None] * nblk
            cur[blk] = acc
            u_blocks[p][blk] = _bdot(pick(tinv[p], blk), place(cur))
    u = [place(u_blocks[p]) for p in P]
    y2 = [wy[p][L2:] - _bdot(mrb[p], u[p]) + _bdot(mrk[p], v2[p]) for p in P]
    ys = [y2[p][:L] + y2[p][L:] for p in P]
    upd = [_bdot(v2[p].T, stack2(ks[p] * gtail[p])) - _bdot(u[p].T, stack2(b[p] * gtail[p])) for p in P]
    s_news = [states[p] * jnp.exp(cum_l[p]) + upd[p] for p in P]
    return ys, s_news


def _rwkv_kernel(lw_ref, r_ref, k_ref, v_ref, kk_ref, a_ref, s0_ref, y_ref, sT_ref, s_scr, *, reverse):
    c = pl.program_id(1)

    @pl.when(c == 0)
    def _():
        s_scr[...] = s0_ref[0]

    slabs = [pl.ds(p * LANES, LANES) for p in range(RW_PAIRS)]
    ys, s_news = _rwkv_chunk(*[[ref[0, :, sl] for sl in slabs]
                               for ref in (lw_ref, r_ref, k_ref, v_ref, kk_ref, a_ref)],
                             [s_scr[p] for p in range(RW_PAIRS)], reverse)
    for p in range(RW_PAIRS):
        y_ref[0, :, slabs[p]] = ys[p]
        s_scr[p] = s_news[p]

    @pl.when(c == pl.num_programs(1) - 1)
    def _():
        sT_ref[0] = s_scr[...]


def rwkv7_scan(rkv, lw, k, kk, a, S0, reverse=False):
    B, T, C = lw.shape
    assert C == RW_PAIRS * LANES and T % RW_L == 0 and rkv.shape[-1] >= 3 * C
    nc = T // RW_L
    chunk = (lambda c: nc - 1 - c) if reverse else (lambda c: c)
    seq = pl.BlockSpec((1, RW_L, C), lambda b, c: (b, chunk(c), 0))
    r_spec = seq
    v_spec = pl.BlockSpec((1, RW_L, C), lambda b, c: (b, chunk(c), 2))
    st = pl.BlockSpec((1, RW_PAIRS, LANES, LANES), lambda b, c: (b, 0, 0, 0))
    y, sT = pl.pallas_call(
        functools.partial(_rwkv_kernel, reverse=reverse),
        grid=(B, nc),
        in_specs=[seq, r_spec, seq, v_spec, seq, seq, st],
        out_specs=[seq, st],
        out_shape=[jax.ShapeDtypeStruct((B, T, C), jnp.float32),
                   jax.ShapeDtypeStruct((B, RW_PAIRS, LANES, LANES), jnp.float32)],
        scratch_shapes=[pltpu.VMEM((RW_PAIRS, LANES, LANES), jnp.float32)],
        compiler_params=pltpu.CompilerParams(dimension_semantics=("arbitrary", "arbitrary"),
                                             vmem_limit_bytes=VMEM_LIMIT_BYTES),
        name="rwkv7_scan",
    )(lw, rkv, k, rkv, kk, a, S0)
    return y, sT


def rwkv_stream(p, S0, w0, w_up, a0, a_up, g_up, k_k, k_a, r_k, ln_g, ln_b):
    B, T, _ = p.shape
    r, k, v, wd, ad, gd = split_cols(p, [RW_WIDTH, RW_WIDTH, RW_WIDTH, RW_DECAY_RANK, RW_A_RANK])
    heads = lambda t: t.reshape(B, T, RW_HEADS, RW_HD)
    kk = heads(k * k_k).astype(jnp.float32)
    kk = (kk * lax.rsqrt(jnp.maximum(jnp.sum(kk * kk, -1, keepdims=True), 1e-12))).reshape(B, T, RW_WIDTH)
    g = jax.nn.sigmoid(gd) @ g_up
    per_dir, bonus = [], []
    for d in range(2):
        wl = (w0[d] + jnp.tanh(wd) @ w_up[d]).astype(jnp.float32)
        lw = -jnp.exp(-jax.nn.softplus(-wl) - 0.5)
        a = jax.nn.sigmoid(a0[d] + ad @ a_up[d])
        kd = k * (1 + (a - 1) * k_a)
        per_dir.append(rwkv7_scan(p, lw, kd, kk, a, S0[d], reverse=(d == 1)))
        bonus.append(jnp.sum(heads(r) * heads(kd) * r_k, -1, keepdims=True) * heads(v))
    sT = (per_dir[0][1], per_dir[1][1])
    y = (per_dir[0][0] + per_dir[1][0]).astype(jnp.float32).reshape(B, T, RW_HEADS, RW_HD)
    mu = jnp.mean(y, -1, keepdims=True)
    var = jnp.mean(jnp.square(y - mu), -1, keepdims=True)
    yn = ((y - mu) * lax.rsqrt(var + RW_GN_EPS)).reshape(B, T, RW_WIDTH) * ln_g + ln_b
    out = (yn + (bonus[0] + bonus[1]).reshape(B, T, RW_WIDTH)) * g
    return out.astype(p.dtype), sT


GLA_SUB = 16
GLA_PAIRS = GLA_HEADS // 2
assert GLA_DV == LANES and 2 * GLA_DK == LANES


def _gla_kernel(q_ref, k_ref, g_ref, v_ref, s0_ref, o_ref, sT_ref, s_scr, *, reverse):
    c = pl.program_id(1)
    L = GLA_CHUNK

    @pl.when(c == 0)
    def _():
        s_scr[...] = s0_ref[0]

    row = lax.broadcasted_iota(jnp.int32, (L, L), 0)
    col = lax.broadcasted_iota(jnp.int32, (L, L), 1)
    incl = (col >= row) if reverse else (col <= row)
    last = 0 if reverse else L - 1
    lane = lax.broadcasted_iota(jnp.int32, (1, LANES), 1)
    trow = lax.broadcasted_iota(jnp.int32, (L, LANES), 0)
    outs = []
    for p in range(GLA_PAIRS):
        sl = pl.ds(p * LANES, LANES)
        q, k, g = q_ref[0, :, sl], k_ref[0, :, sl], g_ref[0, :, sl]
        b = jnp.dot(incl.astype(jnp.float32), g, precision=lax.Precision.HIGHEST,
                    preferred_element_type=jnp.float32)
        b_l = b[last:last + 1, :]
        qe = q * jnp.exp(b)
        kd = k * jnp.exp(b_l - b)
        qfs, kfs = [], []
        for blk in range(L // GLA_SUB):
            lo, hi = blk * GLA_SUB, (blk + 1) * GLA_SUB
            first = hi - 1 if reverse else lo
            b_ref = b[first:first + 1, :]
            qfs.append(q[lo:hi, :] * jnp.exp(b[lo:hi, :] - b_ref))
            used = (trow >= lo) if reverse else (trow < hi)
            kfs.append(jnp.where(used, k * jnp.exp(jnp.where(used, b_ref - b, 0.0)), 0.0))
        for hh in range(2):
            h = 2 * p + hh
            m = (lane // GLA_DK) == hh
            v = v_ref[0, :, pl.ds(h * GLA_DV, GLA_DV)]
            s_t = s_scr[h]
            a = jnp.concatenate([_bdot_nt(qf, jnp.where(m, kf, 0.0)) for qf, kf in zip(qfs, kfs)], axis=0)
            o = _bdot(jnp.where(incl, a, 0.0), v) + _bdot_nt(qe, s_t)
            s_new = s_t * jnp.exp(b_l) + _bdot(v.T, jnp.where(m, kd, 0.0))
            outs.append((h, o, s_new))
    for h, o, s_new in outs:
        o_ref[0, :, pl.ds(h * GLA_DV, GLA_DV)] = o
        s_scr[h] = s_new

    @pl.when(c == pl.num_programs(1) - 1)
    def _():
        sT_ref[0] = s_scr[...]


def gla_chunked(q, k, g, v, S0, reverse=False):
    B, T, _ = q.shape
    assert T % GLA_CHUNK == 0
    nc = T // GLA_CHUNK
    chunk = (lambda b, c: (b, nc - 1 - c, 0)) if reverse else (lambda b, c: (b, c, 0))
    qk = pl.BlockSpec((1, GLA_CHUNK, GLA_HEADS * GLA_DK), chunk)
    vv = pl.BlockSpec((1, GLA_CHUNK, GLA_HEADS * GLA_DV), chunk)
    st = pl.BlockSpec((1, GLA_HEADS, LANES, LANES), lambda b, c: (b, 0, 0, 0))
    return pl.pallas_call(
        functools.partial(_gla_kernel, reverse=reverse),
        grid=(B, nc),
        in_specs=[qk, qk, qk, vv, st],
        out_specs=[vv, st],
        out_shape=[jax.ShapeDtypeStruct((B, T, GLA_HEADS * GLA_DV), jnp.float32),
                   jax.ShapeDtypeStruct((B, GLA_HEADS, LANES, LANES), jnp.float32)],
        scratch_shapes=[pltpu.VMEM((GLA_HEADS, LANES, LANES), jnp.float32)],
        compiler_params=pltpu.CompilerParams(dimension_semantics=("arbitrary", "arbitrary"),
                                             vmem_limit_bytes=VMEM_LIMIT_BYTES),
        name="gla_chunked",
    )(q, k, g, v, S0)


def gla_stream(p, S0, gate_up, gate_b, norm_g):
    B, T, _ = p.shape
    q, k, v, gd, og = split_cols(p, [GLA_HEADS * GLA_DK, GLA_HEADS * GLA_DK, GLA_HEADS * GLA_DV, GLA_GATE_RANK])
    q = q * GLA_DK ** -0.5
    outs, states = [], []
    for d in range(2):
        glog = jax.nn.log_sigmoid((gd @ gate_up[d] + gate_b[d]).astype(jnp.float32)) / GLA_GATE_NORM
        od, sd = gla_chunked(q, k, glog, v, S0[d], reverse=(d == 1))
        outs.append(od)
        states.append(sd)
    o = (outs[0] + outs[1]).reshape(B, T, GLA_HEADS, GLA_DV)
    o = rmsnorm(o, norm_g.reshape(GLA_HEADS, GLA_DV)).reshape(B, T, GLA_HEADS * GLA_DV)
    return (o * jax.nn.silu(og)).astype(p.dtype), tuple(states)


def ssd_chunked(x, dt, A, Bm, Cm, S0):
    Bsz, T, H, P = x.shape
    N = Bm.shape[-1]
    L = SSD_CHUNK
    n = T // L
    f = lambda t: t.astype(jnp.float32)
    xc = (f(x) * f(dt)[..., None]).reshape(Bsz, n, L, H, P)
    Bc = f(Bm).reshape(Bsz, n, L, H, N)
    Cc = f(Cm).reshape(Bsz, n, L, H, N)
    acs = jnp.cumsum((f(dt) * A).reshape(Bsz, n, L, H).transpose(0, 3, 1, 2), -1)
    mask = jnp.tril(jnp.ones((L, L), bool))
    Lmat = jnp.exp(jnp.where(mask, acs[..., :, None] - acs[..., None, :], -jnp.inf))
    G = jnp.einsum("bclhn,bcshn->bhcls", Cc, Bc) * Lmat
    y_diag = jnp.einsum("bhcls,bcshp->bclhp", G, xc)
    contrib = jnp.einsum("bclhn,bhcl,bclhp->bchpn", Bc, jnp.exp(acs[..., -1:] - acs), xc)
    chunk_decay = jnp.exp(acs[..., -1])

    def step(S, inp):
        dec, st = inp
        return dec[..., None, None] * S + st, S

    S, S_in = lax.scan(step, S0, (jnp.moveaxis(chunk_decay, 2, 0), jnp.moveaxis(contrib, 1, 0)))
    y_off = jnp.einsum("bclhn,bchpn,bhcl->bclhp", Cc, jnp.moveaxis(S_in, 0, 1), jnp.exp(acs))
    return (y_diag + y_off).reshape(Bsz, T, H, P).astype(x.dtype), S


def ssd_stream(p, S0f, S0b, conv_w, conv_b, dt_bias, A_log, D_skip, norm_g):
    B, T, _ = p.shape
    z, xbc, dtr = split_cols(p, [SSD_INNER, SSD_CONV_CH])
    xbc = jax.nn.silu(dwconv(xbc, conv_w, conv_b))
    xs, Bm, Cm = split_cols(xbc, [SSD_INNER, SSD_GROUPS * SSD_N])
    xs = xs.reshape(B, T, SSD_HEADS, SSD_P)
    rep = SSD_HEADS // SSD_GROUPS
    Bm = jnp.repeat(Bm.reshape(B, T, SSD_GROUPS, SSD_N), rep, axis=2)
    Cm = jnp.repeat(Cm.reshape(B, T, SSD_GROUPS, SSD_N), rep, axis=2)
    ys, states = [], []
    for d, S0 in enumerate((S0f, S0b)):
        dt = jax.nn.softplus((dtr + dt_bias[d]).astype(jnp.float32))
        A = -jnp.exp(A_log[d].astype(jnp.float32))
        args = [xs, dt, Bm, Cm]
        if d == 1:
            args = [flip(t) for t in args]
        yd, Sd = ssd_chunked(args[0], args[1], A, args[2], args[3], S0)
        ys.append(flip(yd) if d == 1 else yd)
        states.append(Sd)
    y = (ys[0] + ys[1] + xs * D_skip[:, None]).reshape(B, T, SSD_INNER) * jax.nn.silu(z)
    return rmsnorm(y, norm_g).astype(p.dtype), states[0], states[1]


def linear_scan(a, b, h0):
    b = b.at[:, 0].add(a[:, 0] * h0)
    comb = lambda l, r: (l[0] * r[0], r[0] * l[1] + r[1])
    _, h = lax.associative_scan(comb, (a, b), axis=1)
    return h, h[:, -1]


def lru_stream(p, h0f, h0b, conv_w, conv_b, wa, ba, wx, bx, lam):
    B, T, _ = p.shape
    xr, gate = split_cols(p, [LRU_WIDTH])
    xf = dwconv(xr, conv_w, conv_b).astype(jnp.float32)
    xh = xf.reshape(B, T, LRU_BLOCKS, LRU_BD)
    hs, finals = [], []
    for d, h0 in enumerate((h0f, h0b)):
        r = jax.nn.sigmoid(jnp.einsum("btgi,gij->btgj", xh, wa[d]).reshape(B, T, LRU_WIDTH) + ba[d])
        i = jax.nn.sigmoid(jnp.einsum("btgi,gij->btgj", xh, wx[d]).reshape(B, T, LRU_WIDTH) + bx[d])
        log_a = -LRU_C * r * jax.nn.softplus(-lam[d])
        a = jnp.exp(log_a)
        b = jnp.sqrt(-jnp.expm1(2 * log_a)) * (i * xf)
        if d == 1:
            a, b = flip(a), flip(b)
        h, hT = linear_scan(a, b, h0)
        hs.append(flip(h) if d == 1 else h)
        finals.append(hT)
    y = (hs[0] + hs[1]).astype(p.dtype) * jax.nn.gelu(gate)
    return y, finals[0], finals[1]


def even_mixer(u, uc, w_in, mu, rw_p, gla_p):
    B = u.shape[0]
    p, pc = mm3(u, w_in), mm3(uc, w_in)
    rw, gl = p[..., :RW_COLS], p[..., RW_COLS:]
    rwc, glc = pc[..., :RW_COLS], pc[..., RW_COLS:]
    rw = rw + mu * (qshift_grid(rw) - rw)
    rwc = rwc + mu * (shift_seq(rwc) - rwc)
    z_rw = jnp.zeros((B, RW_PAIRS, LANES, LANES), jnp.float32)
    yc_rw, s_ctx = rwkv_stream(rwc, (z_rw, z_rw), *rw_p)
    y_rw, _ = rwkv_stream(rw, s_ctx, *rw_p)
    z_gla = jnp.zeros((B, GLA_HEADS, LANES, LANES), jnp.float32)
    yc_gl, g_ctx = gla_stream(glc, (z_gla, z_gla), *gla_p)
    y_gl, _ = gla_stream(gl, g_ctx, *gla_p)
    return (y_rw, y_gl), (yc_rw, yc_gl)


def odd_mixer(u, uc, w_in, ssd_p, lru_p):
    B = u.shape[0]
    p = grid_to_cols(mm3(u, w_in))
    pc = mm3(uc, w_in)
    z_s = jnp.zeros((B, SSD_HEADS, SSD_P, SSD_N), jnp.float32)
    yc_s, Sf, Sb = ssd_stream(pc[..., :SSD_COLS], z_s, z_s, *ssd_p)
    y_s, _, _ = ssd_stream(p[..., :SSD_COLS], Sf, Sb, *ssd_p)
    z_l = jnp.zeros((B, LRU_WIDTH), jnp.float32)
    yc_l, hf, hb = lru_stream(pc[..., SSD_COLS:], z_l, z_l, *lru_p)
    y_l, _, _ = lru_stream(p[..., SSD_COLS:], hf, hb, *lru_p)
    return (cols_to_grid(y_s), cols_to_grid(y_l)), (yc_s, yc_l)


def peer(x, wq, keys, u_tab, v_tab):
    B, T, D = x.shape
    M = B * T
    xf = x.reshape(M, D)
    idx, gate_t = peer_route(xf, wq, keys)
    return peer_experts(xf, idx, gate_t, u_tab, v_tab).reshape(B, T, D)


def kernel(x, c, ctx, c_ctx, ada_w, ada_b, norm1_g, norm2_g, ev_w_in, rw_mu, rw_w0, rw_w_up, rw_a0, rw_a_up,
           rw_g_up, rw_k_k, rw_k_a, rw_r_k, rw_ln_g, rw_ln_b, gla_gate_up, gla_gate_b, gla_norm_g, od_w_in,
           ssd_conv_w, ssd_conv_b, ssd_dt_bias, ssd_A_log, ssd_D, ssd_norm_g, lru_conv_w, lru_conv_b, lru_wa,
           lru_ba, lru_wx, lru_bx, lru_lam, w_out, peer_wq, peer_keys, peer_u, peer_v, final_g):
    h, hc = x, ctx
    sc, scc = jax.nn.silu(c), jax.nn.silu(c_ctx)
    for l in range(DEPTH):
        last = l == DEPTH - 1
        i = l // 2
        mod = jnp.split((sc @ ada_w[l] + ada_b[l])[:, None, :], 6, axis=-1)
        modc = jnp.split(scc @ ada_w[l] + ada_b[l], 6, axis=-1)
        u = modulate(h, norm1_g[l], mod[0], mod[1])
        uc = modulate(hc, norm1_g[l], modc[0], modc[1])
        if l % 2 == 0:
            rw_p = (rw_w0[i], rw_w_up[i], rw_a0[i], rw_a_up[i], rw_g_up[i], rw_k_k[i], rw_k_a[i], rw_r_k[i],
                    rw_ln_g[i], rw_ln_b[i])
            gla_p = (gla_gate_up[i], gla_gate_b[i], gla_norm_g[i])
            y, yc = even_mixer(u, uc, ev_w_in[i], rw_mu[i], rw_p, gla_p)
        else:
            ssd_p = (ssd_conv_w[i], ssd_conv_b[i], ssd_dt_bias[i], ssd_A_log[i], ssd_D[i], ssd_norm_g[i])
            lru_p = (lru_conv_w[i], lru_conv_b[i], lru_wa[i], lru_ba[i], lru_wx[i], lru_bx[i], lru_lam[i])
            y, yc = odd_mixer(u, uc, od_w_in[i], ssd_p, lru_p)
        h = h + mod[2] * mm3(y, w_out[l])
        u = modulate(h, norm2_g[l], mod[3], mod[4])
        h = h + mod[5] * peer(u, peer_wq[l], peer_keys[l], peer_u[l], peer_v[l])
        if not last:
            hc = hc + modc[2] * mm3(yc, w_out[l])
            uc = modulate(hc, norm2_g[l], modc[3], modc[4])
            hc = hc + modc[5] * peer(uc, peer_wq[l], peer_keys[l], peer_u[l], peer_v[l])
    return rmsnorm(h, final_g)
```
